```python
import math
import jax, jax.numpy as jnp
from jax import lax
import numpy as np

D_MODEL = 1024
BATCH = 8
SEQ = 8192
DEPTH = 4

N_EVEN = (DEPTH + 1) // 2
N_ODD = DEPTH // 2

MEM_LEN = 256
XA_HEADS = 4
XA_HEAD_DIM = D_MODEL // XA_HEADS

POOL_W = D_MODEL // 2
POOL_WINDOWS = (2, 4, 8, 16)
N_POOL_GROUPS = len(POOL_WINDOWS)
POOL_GROUP = POOL_W // N_POOL_GROUPS
CONV_W = D_MODEL // 2
CONV_K = 31

MLA_HEADS = 16
QK_NOPE = 64
QK_ROPE = 32
V_HEAD = 64
Q_LORA = 384
KV_LORA = 256
ROPE_THETA = 10000.0
Q_BLOCK = 128
MLA_SCALE = 1.0 / math.sqrt(QK_NOPE + QK_ROPE)

D_FF = 2816
FFN_CONV_K = 3

EPS = 1e-6
NEG = -1e30

kernel_name = "hybrid_pool_conv_mla_memxattn_convffn"


def rmsnorm(x, g):
    xf = x.astype(jnp.float32)
    y = xf * lax.rsqrt(jnp.mean(xf * xf, axis=-1, keepdims=True) + EPS)
    return (y * g.astype(jnp.float32)).astype(x.dtype)


def layernorm(x, g, b):
    xf = x.astype(jnp.float32)
    mu = jnp.mean(xf, axis=-1, keepdims=True)
    xc = xf - mu
    y = xc * lax.rsqrt(jnp.mean(xc * xc, axis=-1, keepdims=True) + EPS)
    return (y * g.astype(jnp.float32) + b.astype(jnp.float32)).astype(x.dtype)


def causal_dwconv(u, w):
    k = w.shape[0]
    return lax.conv_general_dilated(
        u, w[:, None, :], window_strides=(1,), padding=[(k - 1, 0)],
        dimension_numbers=("NWC", "WIO", "NWC"), feature_group_count=u.shape[-1])


def window_mean_minus_self(u, w):
    t = u.shape[1]
    uf = u.astype(jnp.float32)
    cs = jnp.cumsum(uf, axis=1)
    cs_lag = jnp.pad(cs, ((0, 0), (w, 0), (0, 0)))[:, :t]
    cnt = jnp.minimum(jnp.arange(t) + 1, w).astype(jnp.float32)
    return ((cs - cs_lag) / cnt[None, :, None] - uf).astype(u.dtype)


def rope_tables(positions):
    inv = 1.0 / (ROPE_THETA ** (jnp.arange(0, QK_ROPE, 2, dtype=jnp.float32) / QK_ROPE))
    ang = positions.astype(jnp.float32)[..., None] * inv
    return jnp.cos(ang), jnp.sin(ang)


def apply_rope(x, cos, sin):
    half = x.shape[-1] // 2
    c = cos.astype(x.dtype)
    s = sin.astype(x.dtype)
    x1, x2 = x[..., :half], x[..., half:]
    return jnp.concatenate([x1 * c - x2 * s, x1 * s + x2 * c], axis=-1)


def pool_conv_mixer(h, w_in, pool_w, pool_scale, dw_w, dw_b, ln_g, ln_b, w_out):
    b, t, _ = h.shape
    z = h @ w_in
    u, glu_a, glu_b = jnp.split(z, [POOL_W, POOL_W + CONV_W], axis=-1)
    ug = u.reshape(b, t, N_POOL_GROUPS, POOL_GROUP)
    pooled = jnp.stack([window_mean_minus_self(ug[:, :, i], w)
                        for i, w in enumerate(POOL_WINDOWS)], axis=2)
    ya = jnp.einsum("btgc,gcd->btgd", pooled, pool_w).reshape(b, t, POOL_W) * pool_scale
    gl = glu_a * jax.nn.sigmoid(glu_b)
    cv = causal_dwconv(gl, dw_w) + dw_b
    yb = jax.nn.silu(layernorm(cv, ln_g, ln_b))
    return jnp.concatenate([ya, yb], axis=-1) @ w_out


def mla_attention(h, cos, sin, w_dq_dkv, q_norm_g, w_uq, kv_norm_g, w_ukv, w_o):
    b, t, _ = h.shape
    c = h @ w_dq_dkv
    cq, ckv, k_pe = jnp.split(c, [Q_LORA, Q_LORA + KV_LORA], axis=-1)
    q = (rmsnorm(cq, q_norm_g) @ w_uq).reshape(b, t, MLA_HEADS, QK_NOPE + QK_ROPE)
    q_nope = q[..., :QK_NOPE]
    q_pe = apply_rope(q[..., QK_NOPE:], cos[:, :, None, :], sin[:, :, None, :])
    kv = (rmsnorm(ckv, kv_norm_g) @ w_ukv).reshape(b, t, MLA_HEADS, QK_NOPE + V_HEAD)
    k_nope, v = kv[..., :QK_NOPE], kv[..., QK_NOPE:]
    k_pe = apply_rope(k_pe, cos, sin)
    nb = t // Q_BLOCK
    kpos = jnp.arange(t)

    def block(args):
        qn, qp, i = args
        s = (jnp.einsum("bqhd,bkhd->bhqk", qn, k_nope)
             + jnp.einsum("bqhr,bkr->bhqk", qp, k_pe)).astype(jnp.float32) * MLA_SCALE
        qpos = i * Q_BLOCK + jnp.arange(Q_BLOCK)
        s = jnp.where(kpos[None, :] <= qpos[:, None], s, NEG)
        p = jax.nn.softmax(s, axis=-1).astype(v.dtype)
        return jnp.einsum("bhqk,bkhd->bqhd", p, v)

    qn_b = q_nope.reshape(b, nb, Q_BLOCK, MLA_HEADS, QK_NOPE).transpose(1, 0, 2, 3, 4)
    qp_b = q_pe.reshape(b, nb, Q_BLOCK, MLA_HEADS, QK_ROPE).transpose(1, 0, 2, 3, 4)
    o = lax.map(block, (qn_b, qp_b, jnp.arange(nb)))
    o = o.transpose(1, 0, 2, 3, 4).reshape(b, t, MLA_HEADS * V_HEAD)
    return o @ w_o


def memory_cross_attention(h, m, wq, wkv, wo):
    b, t, _ = h.shape
    q = (h @ wq).reshape(b, t, XA_HEADS, XA_HEAD_DIM)
    k, v = jnp.split(m @ wkv, 2, axis=-1)
    k = k.reshape(b, MEM_LEN, XA_HEADS, XA_HEAD_DIM)
    v = v.reshape(b, MEM_LEN, XA_HEADS, XA_HEAD_DIM)
    s = jnp.einsum("bthd,bmhd->bhtm", q, k).astype(jnp.float32) * (XA_HEAD_DIM ** -0.5)
    p = jax.nn.softmax(s, axis=-1).astype(v.dtype)
    o = jnp.einsum("bhtm,bmhd->bthd", p, v).reshape(b, t, D_MODEL)
    return o @ wo


def conv_ffn(h, w_up, conv_w, conv_b, w_down):
    a, g = jnp.split(h @ w_up, 2, axis=-1)
    g = causal_dwconv(g, conv_w) + conv_b
    return (jax.nn.silu(g) * a) @ w_down


def setup_inputs(seed: int = 0) -> dict:
    key = jax.random.key(seed)
    ks = iter(jax.random.split(key, 40))
    f32 = jnp.float32

    def dense(shape, fan_in, scale=1.0):
        return jax.random.normal(next(ks), shape, f32) * (scale * fan_in ** -0.5)

    def gain(shape):
        return 1.0 + 0.02 * jax.random.normal(next(ks), shape, f32)

    def bias(shape):
        return 0.01 * jax.random.normal(next(ks), shape, f32)

    out_scale = 0.5
    x = jax.random.normal(next(ks), (BATCH, SEQ, D_MODEL), f32)
    mem = jax.random.normal(next(ks), (BATCH, MEM_LEN, D_MODEL), f32)
    offsets = jax.random.randint(next(ks), (BATCH, 1), 0, 4096, dtype=jnp.int32)
    positions = offsets + jnp.arange(SEQ, dtype=jnp.int32)[None, :]
    return {
        "x": x,
        "mem": mem,
        "positions": positions,
        "norm_mix_g": gain((DEPTH, D_MODEL)),
        "norm_xa_g": gain((DEPTH, D_MODEL)),
        "norm_mem_g": gain((DEPTH, D_MODEL)),
        "xa_wq": dense((DEPTH, D_MODEL, D_MODEL), D_MODEL),
        "xa_wkv": dense((DEPTH, D_MODEL, 2 * D_MODEL), D_MODEL),
        "xa_wo": dense((DEPTH, D_MODEL, D_MODEL), D_MODEL, out_scale),
        "norm_ffn_g": gain((DEPTH, D_MODEL)),
        "ffn_w_up": dense((DEPTH, D_MODEL, 2 * D_FF), D_MODEL),
        "ffn_conv_w": dense((DEPTH, FFN_CONV_K, D_FF), FFN_CONV_K),
        "ffn_conv_b": bias((DEPTH, D_FF)),
        "ffn_w_down": dense((DEPTH, D_FF, D_MODEL), D_FF, out_scale),
        "pc_w_in": dense((N_EVEN, D_MODEL, POOL_W + 2 * CONV_W), D_MODEL),
        "pool_w": dense((N_EVEN, N_POOL_GROUPS, POOL_GROUP, POOL_GROUP), POOL_GROUP),
        "pool_scale": gain((N_EVEN, POOL_W)),
        "conv_dw_w": dense((N_EVEN, CONV_K, CONV_W), CONV_K),
        "conv_dw_b": bias((N_EVEN, CONV_W)),
        "conv_ln_g": gain((N_EVEN, CONV_W)),
        "conv_ln_b": bias((N_EVEN, CONV_W)),
        "pc_w_out": dense((N_EVEN, POOL_W + CONV_W, D_MODEL), POOL_W + CONV_W, out_scale),
        "mla_w_dq_dkv": dense((N_ODD, D_MODEL, Q_LORA + KV_LORA + QK_ROPE), D_MODEL),
        "mla_q_norm_g": gain((N_ODD, Q_LORA)),
        "mla_w_uq": dense((N_ODD, Q_LORA, MLA_HEADS * (QK_NOPE + QK_ROPE)), Q_LORA),
        "mla_kv_norm_g": gain((N_ODD, KV_LORA)),
        "mla_w_ukv": dense((N_ODD, KV_LORA, MLA_HEADS * (QK_NOPE + V_HEAD)), KV_LORA),
        "mla_w_o": dense((N_ODD, MLA_HEADS * V_HEAD, D_MODEL), MLA_HEADS * V_HEAD, out_scale),
        "final_norm_g": gain((D_MODEL,)),
    }


def reference(x, mem, positions, norm_mix_g, norm_xa_g, norm_mem_g, xa_wq, xa_wkv, xa_wo,
              norm_ffn_g, ffn_w_up, ffn_conv_w, ffn_conv_b, ffn_w_down,
              pc_w_in, pool_w, pool_scale, conv_dw_w, conv_dw_b, conv_ln_g, conv_ln_b, pc_w_out,
              mla_w_dq_dkv, mla_q_norm_g, mla_w_uq, mla_kv_norm_g, mla_w_ukv, mla_w_o,
              final_norm_g):
    cos, sin = rope_tables(positions)
    for l in range(DEPTH):
        h = rmsnorm(x, norm_mix_g[l])
        if l % 2 == 0:
            e = l // 2
            x = x + pool_conv_mixer(h, pc_w_in[e], pool_w[e], pool_scale[e], conv_dw_w[e],
                                    conv_dw_b[e], conv_ln_g[e], conv_ln_b[e], pc_w_out[e])
        else:
            o = l // 2
            x = x + mla_attention(h, cos, sin, mla_w_dq_dkv[o], mla_q_norm_g[o], mla_w_uq[o],
                                  mla_kv_norm_g[o], mla_w_ukv[o], mla_w_o[o])
        x = x + memory_cross_attention(rmsnorm(x, norm_xa_g[l]), rmsnorm(mem, norm_mem_g[l]),
                                       xa_wq[l], xa_wkv[l], xa_wo[l])
        x = x + conv_ffn(rmsnorm(x, norm_ffn_g[l]), ffn_w_up[l], ffn_conv_w[l], ffn_conv_b[l],
                         ffn_w_down[l])
    return rmsnorm(x, final_norm_g)
```

```python
import functools
import math

import jax
import jax.numpy as jnp
import numpy as np
from jax import lax
from jax.experimental import pallas as pl
from jax.experimental.pallas import tpu as pltpu

D_MODEL = 1024
MEM_LEN = 256
XA_HEADS = 4
XA_HEAD_DIM = D_MODEL // XA_HEADS
POOL_W = 512
POOL_WINDOWS = (2, 4, 8, 16)
POOL_GROUP = 128
CONV_W = 512
CONV_K = 31
MLA_HEADS = 16
QK_NOPE = 64
QK_ROPE = 32
V_HEAD = 64
Q_LORA = 384
KV_LORA = 256
ROPE_THETA = 10000.0
MLA_SCALE = 1.0 / math.sqrt(QK_NOPE + QK_ROPE)
D_FF = 2816
FFN_CONV_K = 3
EPS = 1e-6
NEG = -1e30
LOG2E = 1.4426950408889634

LANES = 128
SUBLANES = 8
VMEM_LIMIT = 56 * 1024 * 1024

TB = 512
FF_CHUNK = 256
POOL_HALO = 16
CONV_HALO = 32
BQ = 512
BK = 512
HEAD_TILE = LANES

BF16 = jnp.bfloat16
F32 = jnp.float32


def _cparams(sem):
    return pltpu.CompilerParams(dimension_semantics=sem, vmem_limit_bytes=VMEM_LIMIT)


def _rms(x, g):
    return x * lax.rsqrt(jnp.mean(x * x, axis=-1, keepdims=True) + EPS) * g


def _dot(a, b):
    return jnp.dot(a, b, preferred_element_type=F32)


def _dot_nt(a, b):
    return lax.dot_general(a, b, (((1,), (1,)), ((), ())), preferred_element_type=F32)


def _const_spec(shape):
    n = len(shape)
    return pl.BlockSpec(shape, lambda *_: (0,) * n, pipeline_mode=pl.Buffered(1))


def _mem_kv_kernel(mem_ref, g_ref, wkv_ref, k_ref, v_ref):
    m = _rms(mem_ref[0], g_ref[0]).astype(BF16)
    kv = _dot(m, wkv_ref[0])
    k_ref[0, 0] = kv[:, :D_MODEL].astype(BF16)
    v_ref[0, 0] = kv[:, D_MODEL:].astype(BF16)


def _mem_kv(mem, norm_mem_g, xa_wkv):
    depth = xa_wkv.shape[0]
    b = mem.shape[0]
    out = jax.ShapeDtypeStruct((depth, b, MEM_LEN, D_MODEL), BF16)
    return pl.pallas_call(
        _mem_kv_kernel,
        out_shape=(out, out),
        grid=(depth, b),
        in_specs=[
            pl.BlockSpec((1, MEM_LEN, D_MODEL), lambda l, i: (i, 0, 0)),
            pl.BlockSpec((1, 1, D_MODEL), lambda l, i: (l, 0, 0)),
            pl.BlockSpec((1, D_MODEL, 2 * D_MODEL), lambda l, i: (l, 0, 0)),
        ],
        out_specs=(
            pl.BlockSpec((1, 1, MEM_LEN, D_MODEL), lambda l, i: (l, i, 0, 0)),
            pl.BlockSpec((1, 1, MEM_LEN, D_MODEL), lambda l, i: (l, i, 0, 0)),
        ),
        compiler_params=_cparams(("arbitrary", "arbitrary")),
        name="mem_kv",
    )(mem, norm_mem_g.reshape(depth, 1, D_MODEL), xa_wkv)


def _xattn_kernel(*refs, with_attn_proj):
    if with_attn_proj:
        x_ref, o_ref, wo_mla_ref, g_ref, wq_ref, k_ref, v_ref, wo_ref, out_ref = refs
        x = x_ref[0] + _dot(o_ref[0], wo_mla_ref[...])
    else:
        x_ref, g_ref, wq_ref, k_ref, v_ref, wo_ref, out_ref = refs
        x = x_ref[0]
    h = _rms(x, g_ref[...]).astype(BF16)
    q = (_dot(h, wq_ref[...]) * (XA_HEAD_DIM ** -0.5)).astype(BF16)
    heads = []
    for hd in range(XA_HEADS):
        sl = slice(hd * XA_HEAD_DIM, (hd + 1) * XA_HEAD_DIM)
        s = _dot_nt(q[:, sl], k_ref[0, :, sl])
        m = jnp.max(s, axis=-1, keepdims=True)
        p = jnp.exp(s - m)
        l = jnp.sum(p, axis=-1, keepdims=True)
        o = _dot(p.astype(BF16), v_ref[0, :, sl]) / l
        heads.append(o.astype(BF16))
    o = jnp.concatenate(heads, axis=-1)
    out_ref[0] = x + _dot(o, wo_ref[...])


def _xattn(x, g, wq, k, v, wo, attn_o=None, wo_mla=None):
    b, t, _ = x.shape
    with_attn_proj = attn_o is not None
    tok = pl.BlockSpec((1, TB, D_MODEL), lambda i, j: (i, j, 0))
    mem = pl.BlockSpec((1, MEM_LEN, D_MODEL), lambda i, j: (i, 0, 0))
    w = _const_spec((D_MODEL, D_MODEL))
    in_specs = [tok]
    args = [x]
    if with_attn_proj:
        in_specs += [tok, w]
        args += [attn_o, wo_mla]
    in_specs += [_const_spec((1, D_MODEL)), w, mem, mem, w]
    args += [g.reshape(1, D_MODEL), wq, k, v, wo]
    return pl.pallas_call(
        functools.partial(_xattn_kernel, with_attn_proj=with_attn_proj),
        out_shape=jax.ShapeDtypeStruct(x.shape, F32),
        grid=(b, t // TB),
        in_specs=in_specs,
        out_specs=tok,
        compiler_params=_cparams(("arbitrary", "arbitrary")),
        name="xattn",
    )(*args)


def _ffn_kernel(x_ref, g_ref, wup_ref, cw_ref, cb_ref, wdn_ref, fg_ref, out_ref,
                gbuf, carry, act, *, final_norm):
    n_chunks = D_FF // FF_CHUNK

    @pl.when(pl.program_id(1) == 0)
    def _():
        carry[...] = jnp.zeros_like(carry)

    x = x_ref[0]
    h = _rms(x, g_ref[...]).astype(BF16)
    for c in range(n_chunks):
        cs = slice(c * FF_CHUNK, (c + 1) * FF_CHUNK)
        gs = slice(D_FF + c * FF_CHUNK, D_FF + (c + 1) * FF_CHUNK)
        a = _dot(h, wup_ref[:, cs])
        gt = _dot(h, wup_ref[:, gs])
        gbuf[0:SUBLANES, :] = carry[c]
        gbuf[SUBLANES:SUBLANES + TB, :] = gt
        carry[c] = gbuf[TB:TB + SUBLANES, :]
        cw = cw_ref[:, cs]
        conv = (cw[0:1] * gbuf[SUBLANES - 2:SUBLANES - 2 + TB, :]
                + cw[1:2] * gbuf[SUBLANES - 1:SUBLANES - 1 + TB, :]
                + cw[2:3] * gt + cb_ref[:, cs])
        act[:, cs] = (conv * jax.nn.sigmoid(conv) * a).astype(BF16)
    y = x + _dot(act[...], wdn_ref[...])
    if final_norm:
        y = _rms(y, fg_ref[...])
    out_ref[0] = y


def _ffn(x, g, w_up, conv_w, conv_b, w_down, final_g, final_norm):
    b, t, _ = x.shape
    tok = pl.BlockSpec((1, TB, D_MODEL), lambda i, j: (i, j, 0))
    return pl.pallas_call(
        functools.partial(_ffn_kernel, final_norm=final_norm),
        out_shape=jax.ShapeDtypeStruct(x.shape, F32),
        grid=(b, t // TB),
        in_specs=[
            tok,
            _const_spec((1, D_MODEL)),
            _const_spec((D_MODEL, 2 * D_FF)),
            _const_spec((FFN_CONV_K, D_FF)),
            _const_spec((1, D_FF)),
            _const_spec((D_FF, D_MODEL)),
            _const_spec((1, D_MODEL)),
        ],
        out_specs=tok,
        scratch_shapes=[
            pltpu.VMEM((SUBLANES + TB, FF_CHUNK), F32),
            pltpu.VMEM((D_FF // FF_CHUNK, SUBLANES, FF_CHUNK), F32),
            pltpu.VMEM((TB, D_FF), BF16),
        ],
        compiler_params=_cparams(("arbitrary", "arbitrary")),
        name="conv_ffn",
    )(x, g.reshape(1, D_MODEL), w_up, conv_w, conv_b.reshape(1, D_FF), w_down,
      final_g.reshape(1, D_MODEL))


def _pc_kernel(x_ref, g_ref, win_ref, pw_ref, ps_ref, dw_ref, db_ref, lg_ref, lb_ref, wout_ref,
               out_ref, ubuf, glbuf, ybuf):
    tblk = pl.program_id(1)

    @pl.when(tblk == 0)
    def _():
        ubuf[0:POOL_HALO, :] = jnp.zeros((POOL_HALO, POOL_W), F32)
        glbuf[0:CONV_HALO, :] = jnp.zeros((CONV_HALO, CONV_W), F32)

    x = x_ref[0]
    h = _rms(x, g_ref[...]).astype(BF16)
    z = _dot(h, win_ref[...])
    u = z[:, :POOL_W]
    gl = z[:, POOL_W:POOL_W + CONV_W] * jax.nn.sigmoid(z[:, POOL_W + CONV_W:])
    ubuf[POOL_HALO:POOL_HALO + TB, :] = u
    glbuf[CONV_HALO:CONV_HALO + TB, :] = gl

    pos = tblk * TB + lax.broadcasted_iota(jnp.int32, (TB, 1), 0)
    for gi, w in enumerate(POOL_WINDOWS):
        ls = slice(gi * POOL_GROUP, (gi + 1) * POOL_GROUP)
        ssum = u[:, ls]
        for d in range(1, w):
            ssum = ssum + ubuf[POOL_HALO - d:POOL_HALO - d + TB, ls]
        cnt = jnp.minimum(pos + 1, w).astype(F32)
        pooled = (ssum / cnt - u[:, ls]).astype(BF16)
        ya = _dot(pooled, pw_ref[gi]) * ps_ref[:, ls]
        ybuf[:, ls] = ya.astype(BF16)

    cv = jnp.zeros((TB, CONV_W), F32) + db_ref[...]
    base = CONV_HALO - (CONV_K - 1)
    for j in range(CONV_K):
        cv = cv + dw_ref[j:j + 1, :] * glbuf[base + j:base + j + TB, :]
    mu = jnp.mean(cv, axis=-1, keepdims=True)
    xc = cv - mu
    yn = xc * lax.rsqrt(jnp.mean(xc * xc, axis=-1, keepdims=True) + EPS) * lg_ref[...] + lb_ref[...]
    ybuf[:, POOL_W:] = (yn * jax.nn.sigmoid(yn)).astype(BF16)

    ubuf[0:POOL_HALO, :] = ubuf[TB:TB + POOL_HALO, :]
    glbuf[0:CONV_HALO, :] = glbuf[TB:TB + CONV_HALO, :]

    out_ref[0] = x + _dot(ybuf[...], wout_ref[...])


def _pc_mixer(x, g, w_in, pool_w, pool_scale, dw_w, dw_b, ln_g, ln_b, w_out):
    b, t, _ = x.shape
    tok = pl.BlockSpec((1, TB, D_MODEL), lambda i, j: (i, j, 0))
    return pl.pallas_call(
        _pc_kernel,
        out_shape=jax.ShapeDtypeStruct(x.shape, F32),
        grid=(b, t // TB),
        in_specs=[
            tok,
            _const_spec((1, D_MODEL)),
            _const_spec((D_MODEL, POOL_W + 2 * CONV_W)),
            _const_spec((len(POOL_WINDOWS), POOL_GROUP, POOL_GROUP)),
            _const_spec((1, POOL_W)),
            _const_spec((CONV_K, CONV_W)),
            _const_spec((1, CONV_W)),
            _const_spec((1, CONV_W)),
            _const_spec((1, CONV_W)),
            _const_spec((POOL_W + CONV_W, D_MODEL)),
        ],
        out_specs=tok,
        scratch_shapes=[
            pltpu.VMEM((POOL_HALO + TB, POOL_W), F32),
            pltpu.VMEM((CONV_HALO + TB, CONV_W), F32),
            pltpu.VMEM((TB, POOL_W + CONV_W), BF16),
        ],
        compiler_params=_cparams(("arbitrary", "arbitrary")),
        name="pc_mixer",
    )(x, g.reshape(1, D_MODEL), w_in, pool_w, pool_scale.reshape(1, POOL_W), dw_w,
      dw_b.reshape(1, CONV_W), ln_g.reshape(1, CONV_W), ln_b.reshape(1, CONV_W), w_out)


def _rope_tile(x, cos_t, sin_lo, sin_hi):
    return (x * cos_t + pltpu.roll(x, HEAD_TILE - QK_ROPE // 2, 1) * sin_lo
            + pltpu.roll(x, QK_ROPE // 2, 1) * sin_hi)


def _mla_proj_kernel(x_ref, pos_ref, inv_ref, g_ref, wd_ref, qg_ref, wuq_ref, kvg_ref, wuk_ref,
                     wuv_ref, q_ref, k_ref, v_ref):
    x = x_ref[0]
    h = _rms(x, g_ref[...]).astype(BF16)
    c = _dot(h, wd_ref[...])
    cq = _rms(c[:, :Q_LORA], qg_ref[...]).astype(BF16)
    ckv = _rms(c[:, Q_LORA:Q_LORA + KV_LORA], kvg_ref[...]).astype(BF16)
    kpe = c[:, Q_LORA + KV_LORA:]

    ang = pos_ref[0].astype(F32) * inv_ref[...]
    cos_t = jnp.cos(ang)
    sin_t = jnp.sin(ang)
    lane = lax.broadcasted_iota(jnp.int32, (1, HEAD_TILE), 1)
    half = QK_ROPE // 2
    sin_lo = jnp.where((lane >= QK_NOPE) & (lane < QK_NOPE + half), -sin_t, 0.0)
    sin_hi = jnp.where((lane >= QK_NOPE + half) & (lane < QK_NOPE + 2 * half), sin_t, 0.0)

    kpe = _rope_tile(kpe, cos_t, sin_lo, sin_hi)
    qscale = MLA_SCALE * LOG2E
    q = _dot(cq, wuq_ref[...])
    kn = _dot(ckv, wuk_ref[...])
    for hd in range(MLA_HEADS):
        sl = slice(hd * HEAD_TILE, (hd + 1) * HEAD_TILE)
        q_ref[0, :, sl] = (_rope_tile(q[:, sl], cos_t, sin_lo, sin_hi) * qscale).astype(BF16)
        k_ref[0, :, sl] = (kn[:, sl] + kpe).astype(BF16)
    v_ref[0] = _dot(ckv, wuv_ref[...]).astype(BF16)


def _mla_proj(x, pos3, inv_tile, g, wd, qg, wuq, kvg, wuk, wuv):
    b, t, _ = x.shape
    hw = MLA_HEADS * HEAD_TILE
    dcols = Q_LORA + KV_LORA + HEAD_TILE
    tok = pl.BlockSpec((1, TB, D_MODEL), lambda i, j: (i, j, 0))
    wide = pl.BlockSpec((1, TB, hw), lambda i, j: (i, j, 0))
    return pl.pallas_call(
        _mla_proj_kernel,
        out_shape=(
            jax.ShapeDtypeStruct((b, t, hw), BF16),
            jax.ShapeDtypeStruct((b, t, hw), BF16),
            jax.ShapeDtypeStruct((b, t, MLA_HEADS * V_HEAD), BF16),
        ),
        grid=(b, t // TB),
        in_specs=[
            tok,
            pl.BlockSpec((1, TB, 1), lambda i, j: (i, j, 0)),
            _const_spec((1, HEAD_TILE)),
            _const_spec((1, D_MODEL)),
            _const_spec((D_MODEL, dcols)),
            _const_spec((1, Q_LORA)),
            _const_spec((Q_LORA, hw)),
            _const_spec((1, KV_LORA)),
            _const_spec((KV_LORA, hw)),
            _const_spec((KV_LORA, MLA_HEADS * V_HEAD)),
        ],
        out_specs=(wide, wide, tok),
        compiler_params=_cparams(("arbitrary", "arbitrary")),
        name="mla_proj",
    )(x, pos3, inv_tile, g.reshape(1, D_MODEL), wd, qg.reshape(1, Q_LORA), wuq,
      kvg.reshape(1, KV_LORA), wuk, wuv)


def _flash_kernel(q_ref, k_ref, v_ref, o_ref):
    qi = pl.program_id(2)
    accs = []
    ls = []
    for hd in range(2):
        sl = slice(hd * HEAD_TILE, (hd + 1) * HEAD_TILE)
        q = q_ref[0, :, sl]

        def step(j, carry, masked):
            m, l, acc = carry
            k0 = pl.multiple_of(j * BK, BK)
            s = _dot_nt(q, k_ref[0, pl.ds(k0, BK), sl])
            if masked:
                row = lax.broadcasted_iota(jnp.int32, (BQ, BK), 0)
                col = lax.broadcasted_iota(jnp.int32, (BQ, BK), 1)
                s = jnp.where(col <= row, s, NEG)
            m_new = jnp.maximum(m, jnp.max(s, axis=-1, keepdims=True))
            alpha = jnp.exp2(m - m_new)
            p = jnp.exp2(s - m_new)
            l = alpha * l + jnp.sum(p, axis=-1, keepdims=True)
            acc = alpha * acc + _dot(p.astype(BF16), v_ref[0, pl.ds(k0, BK), :])
            return m_new, l, acc

        init = (jnp.full((BQ, 1), NEG, F32), jnp.zeros((BQ, 1), F32),
                jnp.zeros((BQ, 2 * V_HEAD), F32))
        carry = lax.fori_loop(0, qi, functools.partial(step, masked=False), init)
        _, l, acc = step(qi, carry, True)
        accs.append(acc)
        ls.append(l)
    lane = lax.broadcasted_iota(jnp.int32, (1, 2 * V_HEAD), 1)
    o = jnp.where(lane < V_HEAD, accs[0] / ls[0], accs[1] / ls[1])
    o_ref[0] = o.astype(BF16)


def _mla_attn(q, k, v):
    b, t, _ = q.shape
    assert BQ == BK
    return pl.pallas_call(
        _flash_kernel,
        out_shape=jax.ShapeDtypeStruct((b, t, MLA_HEADS * V_HEAD), BF16),
        grid=(b, MLA_HEADS // 2, t // BQ),
        in_specs=[
            pl.BlockSpec((1, BQ, 2 * HEAD_TILE), lambda i, hp, j: (i, j, hp)),
            pl.BlockSpec((1, t, 2 * HEAD_TILE), lambda i, hp, j: (i, 0, hp)),
            pl.BlockSpec((1, t, 2 * V_HEAD), lambda i, hp, j: (i, 0, hp)),
        ],
        out_specs=pl.BlockSpec((1, BQ, 2 * V_HEAD), lambda i, hp, j: (i, j, hp)),
        compiler_params=_cparams(("arbitrary", "arbitrary", "arbitrary")),
        name="mla_attn",
    )(q, k, v)


def _mla_weights(w_dq_dkv, w_uq, w_ukv):
    pad_r = HEAD_TILE - QK_NOPE - QK_ROPE
    wd = jnp.concatenate([
        w_dq_dkv[:, :Q_LORA + KV_LORA],
        jnp.zeros((D_MODEL, QK_NOPE), F32),
        w_dq_dkv[:, Q_LORA + KV_LORA:],
        jnp.zeros((D_MODEL, pad_r), F32)], axis=1)
    wuq = w_uq.reshape(Q_LORA, MLA_HEADS, QK_NOPE + QK_ROPE)
    wuq = jnp.pad(wuq, ((0, 0), (0, 0), (0, pad_r))).reshape(Q_LORA, MLA_HEADS * HEAD_TILE)
    wkv = w_ukv.reshape(KV_LORA, MLA_HEADS, QK_NOPE + V_HEAD)
    wuk = jnp.pad(wkv[:, :, :QK_NOPE], ((0, 0), (0, 0), (0, HEAD_TILE - QK_NOPE)))
    wuk = wuk.reshape(KV_LORA, MLA_HEADS * HEAD_TILE)
    wuv = wkv[:, :, QK_NOPE:].reshape(KV_LORA, MLA_HEADS * V_HEAD)
    return wd.astype(BF16), wuq.astype(BF16), wuk.astype(BF16), wuv.astype(BF16)


def _rope_inv_tile():
    inv = 1.0 / (ROPE_THETA ** (np.arange(0, QK_ROPE, 2, dtype=np.float32) / QK_ROPE))
    tile = np.zeros((1, HEAD_TILE), np.float32)
    half = QK_ROPE // 2
    tile[0, QK_NOPE:QK_NOPE + half] = inv
    tile[0, QK_NOPE + half:QK_NOPE + 2 * half] = inv
    return jnp.asarray(tile)


def kernel(x, mem, positions, norm_mix_g, norm_xa_g, norm_mem_g, xa_wq, xa_wkv, xa_wo, norm_ffn_g, ffn_w_up, ffn_conv_w, ffn_conv_b, ffn_w_down, pc_w_in, pool_w, pool_scale, conv_dw_w, conv_dw_b, conv_ln_g, conv_ln_b, pc_w_out, mla_w_dq_dkv, mla_q_norm_g, mla_w_uq, mla_kv_norm_g, mla_w_ukv, mla_w_o, final_norm_g):
    depth = xa_wq.shape[0]
    b, t, _ = x.shape
    bf = lambda a: a.astype(BF16)
    mem_k, mem_v = _mem_kv(mem, norm_mem_g, bf(xa_wkv))
    xa_wq_b, xa_wo_b = bf(xa_wq), bf(xa_wo)
    w_up_b, w_down_b = bf(ffn_w_up), bf(ffn_w_down)
    pc_w_in_b, pool_w_b, pc_w_out_b = bf(pc_w_in), bf(pool_w), bf(pc_w_out)
    mla_w_o_b = bf(mla_w_o)
    pos3 = positions.reshape(b, t, 1)
    inv_tile = _rope_inv_tile()

    for l in range(depth):
        if l % 2 == 0:
            e = l // 2
            x = _pc_mixer(x, norm_mix_g[l], pc_w_in_b[e], pool_w_b[e], pool_scale[e], conv_dw_w[e],
                          conv_dw_b[e], conv_ln_g[e], conv_ln_b[e], pc_w_out_b[e])
            x = _xattn(x, norm_xa_g[l], xa_wq_b[l], mem_k[l], mem_v[l], xa_wo_b[l])
        else:
            o = l // 2
            wd, wuq, wuk, wuv = _mla_weights(mla_w_dq_dkv[o], mla_w_uq[o], mla_w_ukv[o])
            q, k, v = _mla_proj(x, pos3, inv_tile, norm_mix_g[l], wd, mla_q_norm_g[o], wuq,
                                mla_kv_norm_g[o], wuk, wuv)
            attn = _mla_attn(q, k, v)
            x = _xattn(x, norm_xa_g[l], xa_wq_b[l], mem_k[l], mem_v[l], xa_wo_b[l],
                       attn_o=attn, wo_mla=mla_w_o_b[o])
        x = _ffn(x, norm_ffn_g[l], w_up_b[l], ffn_conv_w[l], ffn_conv_b[l], w_down_b[l],
                 final_norm_g, final_norm=(l == depth - 1))
    return x
```

```python
import functools
import math

import jax
import jax.numpy as jnp
import numpy as np
from jax import lax
from jax.experimental import pallas as pl
from jax.experimental.pallas import tpu as pltpu

D_MODEL = 1024
MEM_LEN = 256
XA_HEADS = 4
XA_HEAD_DIM = D_MODEL // XA_HEADS
POOL_W = 512
POOL_WINDOWS = (2, 4, 8, 16)
POOL_GROUP = 128
CONV_W = 512
CONV_K = 31
MLA_HEADS = 16
QK_NOPE = 64
QK_ROPE = 32
V_HEAD = 64
Q_LORA = 384
KV_LORA = 256
ROPE_THETA = 10000.0
MLA_SCALE = 1.0 / math.sqrt(QK_NOPE + QK_ROPE)
D_FF = 2816
FFN_CONV_K = 3
EPS = 1e-6
NEG = -1e30
LOG2E = 1.4426950408889634

LANES = 128
SUBLANES = 8
VMEM_LIMIT = 56 * 1024 * 1024

TB = 512
FF_CHUNK = 256
POOL_HALO = 16
CONV_HALO = 32
CONV_ROWS = 64
BK = 512
BQ = 2 * BK
HEAD_TILE = LANES
VT_ROWS = V_HEAD + 16

BF16 = jnp.bfloat16
F32 = jnp.float32


def _cparams(sem):
    return pltpu.CompilerParams(dimension_semantics=sem, vmem_limit_bytes=VMEM_LIMIT)


def _rms(x, g):
    return x * lax.rsqrt(jnp.mean(x * x, axis=-1, keepdims=True) + EPS) * g


def _dot(a, b):
    return jnp.dot(a, b, preferred_element_type=F32)


def _dot_nt(a, b):
    return lax.dot_general(a, b, (((1,), (1,)), ((), ())), preferred_element_type=F32)


def _const_spec(shape):
    n = len(shape)
    return pl.BlockSpec(shape, lambda *_: (0,) * n, pipeline_mode=pl.Buffered(1))


def _mem_kv_kernel(mem_ref, g_ref, wkv_ref, k_ref, v_ref):
    m = _rms(mem_ref[0], g_ref[0]).astype(BF16)
    kv = _dot(m, wkv_ref[0])
    k_ref[0, 0] = kv[:, :D_MODEL].astype(BF16)
    v_ref[0, 0] = kv[:, D_MODEL:].astype(BF16)


def _mem_kv(mem, norm_mem_g, xa_wkv):
    depth = xa_wkv.shape[0]
    b = mem.shape[0]
    out = jax.ShapeDtypeStruct((depth, b, MEM_LEN, D_MODEL), BF16)
    return pl.pallas_call(
        _mem_kv_kernel,
        out_shape=(out, out),
        grid=(depth, b),
        in_specs=[
            pl.BlockSpec((1, MEM_LEN, D_MODEL), lambda l, i: (i, 0, 0)),
            pl.BlockSpec((1, 1, D_MODEL), lambda l, i: (l, 0, 0)),
            pl.BlockSpec((1, D_MODEL, 2 * D_MODEL), lambda l, i: (l, 0, 0)),
        ],
        out_specs=(
            pl.BlockSpec((1, 1, MEM_LEN, D_MODEL), lambda l, i: (l, i, 0, 0)),
            pl.BlockSpec((1, 1, MEM_LEN, D_MODEL), lambda l, i: (l, i, 0, 0)),
        ),
        compiler_params=_cparams(("arbitrary", "arbitrary")),
        name="mem_kv",
    )(mem, norm_mem_g.reshape(depth, 1, D_MODEL), xa_wkv)


def _xattn_kernel(*refs, with_attn_proj):
    if with_attn_proj:
        x_ref, o_ref, wo_mla_ref, g_ref, wq_ref, k_ref, v_ref, wo_ref, out_ref = refs
        x = x_ref[0] + _dot(o_ref[0], wo_mla_ref[...])
    else:
        x_ref, g_ref, wq_ref, k_ref, v_ref, wo_ref, out_ref = refs
        x = x_ref[0]
    h = _rms(x, g_ref[...]).astype(BF16)
    q = (_dot(h, wq_ref[...]) * (XA_HEAD_DIM ** -0.5)).astype(BF16)
    heads = []
    for hd in range(XA_HEADS):
        sl = slice(hd * XA_HEAD_DIM, (hd + 1) * XA_HEAD_DIM)
        s = _dot_nt(q[:, sl], k_ref[0, :, sl])
        m = jnp.max(s, axis=-1, keepdims=True)
        p = jnp.exp(s - m)
        l = jnp.sum(p, axis=-1, keepdims=True)
        o = _dot(p.astype(BF16), v_ref[0, :, sl]) / l
        heads.append(o.astype(BF16))
    o = jnp.concatenate(heads, axis=-1)
    out_ref[0] = x + _dot(o, wo_ref[...])


def _xattn(x, g, wq, k, v, wo, attn_o=None, wo_mla=None):
    b, t, _ = x.shape
    with_attn_proj = attn_o is not None
    tok = pl.BlockSpec((1, TB, D_MODEL), lambda i, j: (i, j, 0))
    mem = pl.BlockSpec((1, MEM_LEN, D_MODEL), lambda i, j: (i, 0, 0))
    w = _const_spec((D_MODEL, D_MODEL))
    in_specs = [tok]
    args = [x]
    if with_attn_proj:
        in_specs += [tok, w]
        args += [attn_o, wo_mla]
    in_specs += [_const_spec((1, D_MODEL)), w, mem, mem, w]
    args += [g.reshape(1, D_MODEL), wq, k, v, wo]
    return pl.pallas_call(
        functools.partial(_xattn_kernel, with_attn_proj=with_attn_proj),
        out_shape=jax.ShapeDtypeStruct(x.shape, F32),
        grid=(b, t // TB),
        in_specs=in_specs,
        out_specs=tok,
        compiler_params=_cparams(("arbitrary", "arbitrary")),
        name="xattn",
    )(*args)


def _ffn_kernel(x_ref, g_ref, wup_ref, cw_ref, cb_ref, wdn_ref, fg_ref, out_ref,
                gbuf, carry, act, *, final_norm):
    n_chunks = D_FF // FF_CHUNK

    @pl.when(pl.program_id(1) == 0)
    def _():
        carry[...] = jnp.zeros_like(carry)

    x = x_ref[0]
    h = _rms(x, g_ref[...]).astype(BF16)
    for c in range(n_chunks):
        cs = slice(c * FF_CHUNK, (c + 1) * FF_CHUNK)
        gs = slice(D_FF + c * FF_CHUNK, D_FF + (c + 1) * FF_CHUNK)
        a = _dot(h, wup_ref[:, cs])
        gt = _dot(h, wup_ref[:, gs])
        gbuf[0:SUBLANES, :] = carry[c]
        gbuf[SUBLANES:SUBLANES + TB, :] = gt
        carry[c] = gbuf[TB:TB + SUBLANES, :]
        cw = cw_ref[:, cs]
        conv = (cw[0:1] * gbuf[SUBLANES - 2:SUBLANES - 2 + TB, :]
                + cw[1:2] * gbuf[SUBLANES - 1:SUBLANES - 1 + TB, :]
                + cw[2:3] * gt + cb_ref[:, cs])
        act[:, cs] = (conv * jax.nn.sigmoid(conv) * a).astype(BF16)
    y = x + _dot(act[...], wdn_ref[...])
    if final_norm:
        y = _rms(y, fg_ref[...])
    out_ref[0] = y


def _ffn(x, g, w_up, conv_w, conv_b, w_down, final_g, final_norm):
    b, t, _ = x.shape
    tok = pl.BlockSpec((1, TB, D_MODEL), lambda i, j: (i, j, 0))
    return pl.pallas_call(
        functools.partial(_ffn_kernel, final_norm=final_norm),
        out_shape=jax.ShapeDtypeStruct(x.shape, F32),
        grid=(b, t // TB),
        in_specs=[
            tok,
            _const_spec((1, D_MODEL)),
            _const_spec((D_MODEL, 2 * D_FF)),
            _const_spec((FFN_CONV_K, D_FF)),
            _const_spec((1, D_FF)),
            _const_spec((D_FF, D_MODEL)),
            _const_spec((1, D_MODEL)),
        ],
        out_specs=tok,
        scratch_shapes=[
            pltpu.VMEM((SUBLANES + TB, FF_CHUNK), F32),
            pltpu.VMEM((D_FF // FF_CHUNK, SUBLANES, FF_CHUNK), F32),
            pltpu.VMEM((TB, D_FF), BF16),
        ],
        compiler_params=_cparams(("arbitrary", "arbitrary")),
        name="conv_ffn",
    )(x, g.reshape(1, D_MODEL), w_up, conv_w, conv_b.reshape(1, D_FF), w_down,
      final_g.reshape(1, D_MODEL))


def _pc_kernel(x_ref, g_ref, win_ref, pw_ref, ps_ref, dw_ref, db_ref, lg_ref, lb_ref, wout_ref,
               out_ref, ubuf, glbuf, shbuf, ybuf):
    tblk = pl.program_id(1)

    @pl.when(tblk == 0)
    def _():
        ubuf[0:POOL_HALO, :] = jnp.zeros((POOL_HALO, POOL_W), F32)
        glbuf[0:CONV_HALO, :] = jnp.zeros((CONV_HALO, CONV_W), F32)

    x = x_ref[0]
    h = _rms(x, g_ref[...]).astype(BF16)
    z = _dot(h, win_ref[...])
    u = z[:, :POOL_W]
    gl = z[:, POOL_W:POOL_W + CONV_W] * jax.nn.sigmoid(z[:, POOL_W + CONV_W:])
    ubuf[POOL_HALO:POOL_HALO + TB, :] = u
    glbuf[CONV_HALO:CONV_HALO + TB, :] = gl

    pos = tblk * TB + lax.broadcasted_iota(jnp.int32, (TB, 1), 0)
    for gi, w in enumerate(POOL_WINDOWS):
        ls = slice(gi * POOL_GROUP, (gi + 1) * POOL_GROUP)
        ssum = u[:, ls]
        for d in range(1, w):
            ssum = ssum + ubuf[POOL_HALO - d:POOL_HALO - d + TB, ls]
        cnt = jnp.minimum(pos + 1, w).astype(F32)
        pooled = (ssum / cnt - u[:, ls]).astype(BF16)
        ya = _dot(pooled, pw_ref[gi]) * ps_ref[:, ls]
        ybuf[:, ls] = ya.astype(BF16)

    lo = SUBLANES
    for r in range(1, SUBLANES):
        shbuf[r - 1, lo:CONV_HALO + TB, :] = glbuf[lo - r:CONV_HALO + TB - r, :]
    for c0 in range(0, TB, CONV_ROWS):
        cv = jnp.zeros((CONV_ROWS, CONV_W), F32) + db_ref[...]
        for j in range(CONV_K):
            a, r = divmod(CONV_K - 1 - j, SUBLANES)
            start = CONV_HALO - SUBLANES * a + c0
            src = glbuf if r == 0 else shbuf.at[r - 1]
            cv = cv + dw_ref[j:j + 1, :] * src[start:start + CONV_ROWS, :]
        mu = jnp.mean(cv, axis=-1, keepdims=True)
        xc = cv - mu
        yn = (xc * lax.rsqrt(jnp.mean(xc * xc, axis=-1, keepdims=True) + EPS) * lg_ref[...]
              + lb_ref[...])
        ybuf[c0:c0 + CONV_ROWS, POOL_W:] = (yn * jax.nn.sigmoid(yn)).astype(BF16)

    ubuf[0:POOL_HALO, :] = ubuf[TB:TB + POOL_HALO, :]
    glbuf[0:CONV_HALO, :] = glbuf[TB:TB + CONV_HALO, :]

    out_ref[0] = x + _dot(ybuf[...], wout_ref[...])


def _pc_mixer(x, g, w_in, pool_w, pool_scale, dw_w, dw_b, ln_g, ln_b, w_out):
    b, t, _ = x.shape
    tok = pl.BlockSpec((1, TB, D_MODEL), lambda i, j: (i, j, 0))
    return pl.pallas_call(
        _pc_kernel,
        out_shape=jax.ShapeDtypeStruct(x.shape, F32),
        grid=(b, t // TB),
        in_specs=[
            tok,
            _const_spec((1, D_MODEL)),
            _const_spec((D_MODEL, POOL_W + 2 * CONV_W)),
            _const_spec((len(POOL_WINDOWS), POOL_GROUP, POOL_GROUP)),
            _const_spec((1, POOL_W)),
            _const_spec((CONV_K, CONV_W)),
            _const_spec((1, CONV_W)),
            _const_spec((1, CONV_W)),
            _const_spec((1, CONV_W)),
            _const_spec((POOL_W + CONV_W, D_MODEL)),
        ],
        out_specs=tok,
        scratch_shapes=[
            pltpu.VMEM((POOL_HALO + TB, POOL_W), F32),
            pltpu.VMEM((CONV_HALO + TB, CONV_W), F32),
            pltpu.VMEM((SUBLANES - 1, CONV_HALO + TB, CONV_W), F32),
            pltpu.VMEM((TB, POOL_W + CONV_W), BF16),
        ],
        compiler_params=_cparams(("arbitrary", "arbitrary")),
        name="pc_mixer",
    )(x, g.reshape(1, D_MODEL), w_in, pool_w, pool_scale.reshape(1, POOL_W), dw_w,
      dw_b.reshape(1, CONV_W), ln_g.reshape(1, CONV_W), ln_b.reshape(1, CONV_W), w_out)


def _rope_tile(x, cos_t, sin_lo, sin_hi):
    return (x * cos_t + pltpu.roll(x, HEAD_TILE - QK_ROPE // 2, 1) * sin_lo
            + pltpu.roll(x, QK_ROPE // 2, 1) * sin_hi)


def _mla_proj_kernel(x_ref, pos_ref, inv_ref, g_ref, wd_ref, qg_ref, wuq_ref, kvg_ref, wuk_ref,
                     wuv_ref, q_ref, k_ref, v_ref):
    x = x_ref[0]
    h = _rms(x, g_ref[...]).astype(BF16)
    c = _dot(h, wd_ref[...])
    cq = _rms(c[:, :Q_LORA], qg_ref[...]).astype(BF16)
    ckv = _rms(c[:, Q_LORA:Q_LORA + KV_LORA], kvg_ref[...]).astype(BF16)
    kpe = c[:, Q_LORA + KV_LORA:]

    ang = pos_ref[0].astype(F32) * inv_ref[...]
    cos_t = jnp.cos(ang)
    sin_t = jnp.sin(ang)
    lane = lax.broadcasted_iota(jnp.int32, (1, HEAD_TILE), 1)
    half = QK_ROPE // 2
    sin_lo = jnp.where((lane >= QK_NOPE) & (lane < QK_NOPE + half), -sin_t, 0.0)
    sin_hi = jnp.where((lane >= QK_NOPE + half) & (lane < QK_NOPE + 2 * half), sin_t, 0.0)

    kpe = _rope_tile(kpe, cos_t, sin_lo, sin_hi)
    qscale = MLA_SCALE * LOG2E
    q = _dot(cq, wuq_ref[...])
    kn = _dot(ckv, wuk_ref[...])
    for hd in range(MLA_HEADS):
        sl = slice(hd * HEAD_TILE, (hd + 1) * HEAD_TILE)
        q_ref[0, :, sl] = (_rope_tile(q[:, sl], cos_t, sin_lo, sin_hi) * qscale).astype(BF16)
        k_ref[0, :, sl] = (kn[:, sl] + kpe).astype(BF16)
    vt = _dot_nt(wuv_ref[...], ckv)
    ones_tile = jnp.where(lax.broadcasted_iota(jnp.int32, (VT_ROWS - V_HEAD, TB), 0) == 0, 1.0, 0.0)
    for hd in range(MLA_HEADS):
        v_ref[0, hd, 0, 0:V_HEAD, :] = vt[hd * V_HEAD:(hd + 1) * V_HEAD].astype(BF16)
        v_ref[0, hd, 0, V_HEAD:VT_ROWS, :] = ones_tile.astype(BF16)


def _mla_proj(x, pos3, inv_tile, g, wd, qg, wuq, kvg, wuk, wuv_t):
    b, t, _ = x.shape
    assert TB == BK
    hw = MLA_HEADS * HEAD_TILE
    dcols = Q_LORA + KV_LORA + HEAD_TILE
    tok = pl.BlockSpec((1, TB, D_MODEL), lambda i, j: (i, j, 0))
    wide = pl.BlockSpec((1, TB, hw), lambda i, j: (i, j, 0))
    return pl.pallas_call(
        _mla_proj_kernel,
        out_shape=(
            jax.ShapeDtypeStruct((b, t, hw), BF16),
            jax.ShapeDtypeStruct((b, t, hw), BF16),
            jax.ShapeDtypeStruct((b, MLA_HEADS, t // BK, VT_ROWS, BK), BF16),
        ),
        grid=(b, t // TB),
        in_specs=[
            tok,
            pl.BlockSpec((1, TB, 1), lambda i, j: (i, j, 0)),
            _const_spec((1, HEAD_TILE)),
            _const_spec((1, D_MODEL)),
            _const_spec((D_MODEL, dcols)),
            _const_spec((1, Q_LORA)),
            _const_spec((Q_LORA, hw)),
            _const_spec((1, KV_LORA)),
            _const_spec((KV_LORA, hw)),
            _const_spec((MLA_HEADS * V_HEAD, KV_LORA)),
        ],
        out_specs=(wide, wide,
                   pl.BlockSpec((1, MLA_HEADS, 1, VT_ROWS, BK), lambda i, j: (i, 0, j, 0, 0))),
        compiler_params=_cparams(("arbitrary", "arbitrary")),
        name="mla_proj",
    )(x, pos3, inv_tile, g.reshape(1, D_MODEL), wd, qg.reshape(1, Q_LORA), wuq,
      kvg.reshape(1, KV_LORA), wuk, wuv_t)


ATTN_HEADS_PER_STEP = 2


def _flash_kernel(q_ref, k_ref, vt_ref, o_ref, s_even, s_odd):
    qi = pl.program_id(2)
    nh = ATTN_HEADS_PER_STEP
    units = [(hd, c) for hd in range(nh) for c in range(2)]

    def scores_to(buf, j, u):
        hd, c = units[u]
        lanes = slice(hd * HEAD_TILE, (hd + 1) * HEAD_TILE)
        k0 = pl.multiple_of(j * BK, BK)
        s = _dot_nt(k_ref[0, pl.ds(k0, BK), lanes], q_ref[0, c * BK:(c + 1) * BK, lanes])
        buf[u] = s
        return jnp.max(s, axis=0, keepdims=True)

    def softmax_pv(buf, j, u, m, acc, smax, diagonal=False):
        s = buf[u]
        if diagonal:
            kv = lax.broadcasted_iota(jnp.int32, (BK, BK), 0)
            qq = lax.broadcasted_iota(jnp.int32, (BK, BK), 1)
            s = jnp.where(kv <= qq, s, NEG)
            smax = jnp.max(s, axis=0, keepdims=True)
        m_new = jnp.maximum(m, smax)
        alpha = jnp.exp2(m - m_new)
        p = jnp.exp2(s - m_new).astype(BF16)
        acc = alpha * acc + _dot(vt_ref[0, units[u][0], j], p)
        return m_new, acc

    def half_step(cur, nxt, j, carry):
        ms, accs, smax = (list(t) for t in carry)
        smax_next = []
        for u in range(len(units)):
            smax_next.append(scores_to(nxt, j + 1, u))
            ms[u], accs[u] = softmax_pv(cur, j, u, ms[u], accs[u], smax[u])
        return tuple(ms), tuple(accs), tuple(smax_next)

    def body(jj, carry):
        carry = half_step(s_even, s_odd, 2 * jj, carry)
        return half_step(s_odd, s_even, 2 * jj + 1, carry)

    init = (tuple(jnp.full((1, BK), NEG, F32) for _ in units),
            tuple(jnp.zeros((VT_ROWS, BK), F32) for _ in units),
            tuple(scores_to(s_even, 0, u) for u in range(len(units))))
    ms, accs, smax = lax.fori_loop(0, qi, body, init)

    accs = list(accs)
    for u, (hd, c) in enumerate(units):
        if c == 0:
            _, accs[u] = softmax_pv(s_even, 2 * qi, u, ms[u], accs[u], None, diagonal=True)
        else:
            scores_to(s_odd, 2 * qi + 1, u)
            m, acc = softmax_pv(s_even, 2 * qi, u, ms[u], accs[u], smax[u])
            _, accs[u] = softmax_pv(s_odd, 2 * qi + 1, u, m, acc, None, diagonal=True)
    for c in range(2):
        o_t = jnp.concatenate([accs[u][:V_HEAD] / accs[u][V_HEAD:V_HEAD + 1]
                               for u, (hd, cc) in enumerate(units) if cc == c], axis=0)
        o_ref[0, c * BK:(c + 1) * BK, :] = o_t.T.astype(BF16)


def _mla_attn(q, k, vt):
    b, t, _ = q.shape
    assert BQ == 2 * BK
    nh = ATTN_HEADS_PER_STEP
    return pl.pallas_call(
        _flash_kernel,
        out_shape=jax.ShapeDtypeStruct((b, t, MLA_HEADS * V_HEAD), BF16),
        grid=(b, MLA_HEADS // nh, t // BQ),
        in_specs=[
            pl.BlockSpec((1, BQ, nh * HEAD_TILE), lambda i, hp, j: (i, j, hp)),
            pl.BlockSpec((1, t, nh * HEAD_TILE), lambda i, hp, j: (i, 0, hp)),
            pl.BlockSpec((1, nh, t // BK, VT_ROWS, BK), lambda i, hp, j: (i, hp, 0, 0, 0)),
        ],
        out_specs=pl.BlockSpec((1, BQ, nh * V_HEAD), lambda i, hp, j: (i, j, hp)),
        scratch_shapes=[pltpu.VMEM((2 * nh, BK, BK), F32), pltpu.VMEM((2 * nh, BK, BK), F32)],
        compiler_params=_cparams(("arbitrary", "arbitrary", "arbitrary")),
        name="mla_attn",
    )(q, k, vt)


def _mla_weights(w_dq_dkv, w_uq, w_ukv):
    pad_r = HEAD_TILE - QK_NOPE - QK_ROPE
    wd = jnp.concatenate([
        w_dq_dkv[:, :Q_LORA + KV_LORA],
        jnp.zeros((D_MODEL, QK_NOPE), F32),
        w_dq_dkv[:, Q_LORA + KV_LORA:],
        jnp.zeros((D_MODEL, pad_r), F32)], axis=1)
    wuq = w_uq.reshape(Q_LORA, MLA_HEADS, QK_NOPE + QK_ROPE)
    wuq = jnp.pad(wuq, ((0, 0), (0, 0), (0, pad_r))).reshape(Q_LORA, MLA_HEADS * HEAD_TILE)
    wkv = w_ukv.reshape(KV_LORA, MLA_HEADS, QK_NOPE + V_HEAD)
    wuk = jnp.pad(wkv[:, :, :QK_NOPE], ((0, 0), (0, 0), (0, HEAD_TILE - QK_NOPE)))
    wuk = wuk.reshape(KV_LORA, MLA_HEADS * HEAD_TILE)
    wuv_t = wkv[:, :, QK_NOPE:].reshape(KV_LORA, MLA_HEADS * V_HEAD).T
    return wd.astype(BF16), wuq.astype(BF16), wuk.astype(BF16), wuv_t.astype(BF16)


def _rope_inv_tile():
    inv = 1.0 / (ROPE_THETA ** (np.arange(0, QK_ROPE, 2, dtype=np.float32) / QK_ROPE))
    tile = np.zeros((1, HEAD_TILE), np.float32)
    half = QK_ROPE // 2
    tile[0, QK_NOPE:QK_NOPE + half] = inv
    tile[0, QK_NOPE + half:QK_NOPE + 2 * half] = inv
    return jnp.asarray(tile)


def kernel(x, mem, positions, norm_mix_g, norm_xa_g, norm_mem_g, xa_wq, xa_wkv, xa_wo, norm_ffn_g, ffn_w_up, ffn_conv_w, ffn_conv_b, ffn_w_down, pc_w_in, pool_w, pool_scale, conv_dw_w, conv_dw_b, conv_ln_g, conv_ln_b, pc_w_out, mla_w_dq_dkv, mla_q_norm_g, mla_w_uq, mla_kv_norm_g, mla_w_ukv, mla_w_o, final_norm_g):
    depth = xa_wq.shape[0]
    b, t, _ = x.shape
    bf = lambda a: a.astype(BF16)
    mem_k, mem_v = _mem_kv(mem, norm_mem_g, bf(xa_wkv))
    xa_wq_b, xa_wo_b = bf(xa_wq), bf(xa_wo)
    w_up_b, w_down_b = bf(ffn_w_up), bf(ffn_w_down)
    pc_w_in_b, pool_w_b, pc_w_out_b = bf(pc_w_in), bf(pool_w), bf(pc_w_out)
    mla_w_o_b = bf(mla_w_o)
    pos3 = positions.reshape(b, t, 1)
    inv_tile = _rope_inv_tile()

    for l in range(depth):
        if l % 2 == 0:
            e = l // 2
            x = _pc_mixer(x, norm_mix_g[l], pc_w_in_b[e], pool_w_b[e], pool_scale[e], conv_dw_w[e],
                          conv_dw_b[e], conv_ln_g[e], conv_ln_b[e], pc_w_out_b[e])
            x = _xattn(x, norm_xa_g[l], xa_wq_b[l], mem_k[l], mem_v[l], xa_wo_b[l])
        else:
            o = l // 2
            wd, wuq, wuk, wuv = _mla_weights(mla_w_dq_dkv[o], mla_w_uq[o], mla_w_ukv[o])
            q, k, v = _mla_proj(x, pos3, inv_tile, norm_mix_g[l], wd, mla_q_norm_g[o], wuq,
                                mla_kv_norm_g[o], wuk, wuv)
            attn = _mla_attn(q, k, v)
            x = _xattn(x, norm_xa_g[l], xa_wq_b[l], mem_k[l], mem_v[l], xa_wo_b[l],
                       attn_o=attn, wo_mla=mla_w_o_b[o])
        x = _ffn(x, norm_ffn_g[l], w_up_b[l], ffn_conv_w[l], ffn_conv_b[l], w_down_b[l],
                 final_norm_g, final_norm=(l == depth - 1))
    return x
```

```python
import functools
import math

import jax
import jax.numpy as jnp
import numpy as np
from jax import lax
from jax.experimental import pallas as pl
from jax.experimental.pallas import tpu as pltpu

D_MODEL = 1024
MEM_LEN = 256
XA_HEADS = 4
XA_HEAD_DIM = D_MODEL // XA_HEADS
POOL_W = 512
POOL_WINDOWS = (2, 4, 8, 16)
POOL_GROUP = 128
CONV_W = 512
CONV_K = 31
MLA_HEADS = 16
QK_NOPE = 64
QK_ROPE = 32
V_HEAD = 64
Q_LORA = 384
KV_LORA = 256
ROPE_THETA = 10000.0
MLA_SCALE = 1.0 / math.sqrt(QK_NOPE + QK_ROPE)
D_FF = 2816
FFN_CONV_K = 3
EPS = 1e-6
NEG = -1e30
LOG2E = 1.4426950408889634

LANES = 128
SUBLANES = 8
VMEM_LIMIT = 56 * 1024 * 1024

TB = 512
FF_CHUNK = 256
POOL_HALO = 16
CONV_HALO = 32
CONV_ROWS = 64
BK = 512
BQ = 2 * BK
HEAD_TILE = LANES
ROPE_LANE0 = 48
VT_ROWS = V_HEAD + 16

BF16 = jnp.bfloat16
F32 = jnp.float32


def _cparams(sem):
    return pltpu.CompilerParams(dimension_semantics=sem, vmem_limit_bytes=VMEM_LIMIT)


def _rms(x, g):
    return x * lax.rsqrt(jnp.mean(x * x, axis=-1, keepdims=True) + EPS) * g


def _dot(a, b):
    return jnp.dot(a, b, preferred_element_type=F32)


def _dot_nt(a, b):
    return lax.dot_general(a, b, (((1,), (1,)), ((), ())), preferred_element_type=F32)


def _const_spec(shape):
    n = len(shape)
    return pl.BlockSpec(shape, lambda *_: (0,) * n, pipeline_mode=pl.Buffered(1))


def _mem_kv_kernel(mem_ref, g_ref, wkv_ref, k_ref, v_ref):
    m = _rms(mem_ref[0], g_ref[0]).astype(BF16)
    kv = _dot(m, wkv_ref[0])
    k_ref[0, 0] = kv[:, :D_MODEL].astype(BF16)
    v_ref[0, 0] = kv[:, D_MODEL:].astype(BF16)


def _mem_kv(mem, norm_mem_g, xa_wkv):
    depth = xa_wkv.shape[0]
    b = mem.shape[0]
    out = jax.ShapeDtypeStruct((depth, b, MEM_LEN, D_MODEL), BF16)
    return pl.pallas_call(
        _mem_kv_kernel,
        out_shape=(out, out),
        grid=(depth, b),
        in_specs=[
            pl.BlockSpec((1, MEM_LEN, D_MODEL), lambda l, i: (i, 0, 0)),
            pl.BlockSpec((1, 1, D_MODEL), lambda l, i: (l, 0, 0)),
            pl.BlockSpec((1, D_MODEL, 2 * D_MODEL), lambda l, i: (l, 0, 0)),
        ],
        out_specs=(
            pl.BlockSpec((1, 1, MEM_LEN, D_MODEL), lambda l, i: (l, i, 0, 0)),
            pl.BlockSpec((1, 1, MEM_LEN, D_MODEL), lambda l, i: (l, i, 0, 0)),
        ),
        compiler_params=_cparams(("arbitrary", "arbitrary")),
        name="mem_kv",
    )(mem, norm_mem_g.reshape(depth, 1, D_MODEL), xa_wkv)


def _xattn_kernel(*refs, with_attn_proj):
    if with_attn_proj:
        x_ref, o_ref, wo_mla_ref, g_ref, wq_ref, k_ref, v_ref, wo_ref, out_ref = refs
        x = x_ref[0] + _dot(o_ref[0], wo_mla_ref[...])
    else:
        x_ref, g_ref, wq_ref, k_ref, v_ref, wo_ref, out_ref = refs
        x = x_ref[0]
    h = _rms(x, g_ref[...]).astype(BF16)
    q = (_dot(h, wq_ref[...]) * (XA_HEAD_DIM ** -0.5)).astype(BF16)
    heads = []
    for hd in range(XA_HEADS):
        sl = slice(hd * XA_HEAD_DIM, (hd + 1) * XA_HEAD_DIM)
        s = _dot_nt(q[:, sl], k_ref[0, :, sl])
        m = jnp.max(s, axis=-1, keepdims=True)
        p = jnp.exp(s - m)
        l = jnp.sum(p, axis=-1, keepdims=True)
        o = _dot(p.astype(BF16), v_ref[0, :, sl]) / l
        heads.append(o.astype(BF16))
    o = jnp.concatenate(heads, axis=-1)
    out_ref[0] = x + _dot(o, wo_ref[...])


def _xattn(x, g, wq, k, v, wo, attn_o=None, wo_mla=None):
    b, t, _ = x.shape
    with_attn_proj = attn_o is not None
    tok = pl.BlockSpec((1, TB, D_MODEL), lambda i, j: (i, j, 0))
    mem = pl.BlockSpec((1, MEM_LEN, D_MODEL), lambda i, j: (i, 0, 0))
    w = _const_spec((D_MODEL, D_MODEL))
    in_specs = [tok]
    args = [x]
    if with_attn_proj:
        in_specs += [tok, w]
        args += [attn_o, wo_mla]
    in_specs += [_const_spec((1, D_MODEL)), w, mem, mem, w]
    args += [g.reshape(1, D_MODEL), wq, k, v, wo]
    return pl.pallas_call(
        functools.partial(_xattn_kernel, with_attn_proj=with_attn_proj),
        out_shape=jax.ShapeDtypeStruct(x.shape, F32),
        grid=(b, t // TB),
        in_specs=in_specs,
        out_specs=tok,
        compiler_params=_cparams(("arbitrary", "arbitrary")),
        name="xattn",
    )(*args)


def _ffn_kernel(x_ref, g_ref, wup_ref, cw_ref, cb_ref, wdn_ref, fg_ref, out_ref,
                gbuf, carry, act, *, final_norm):
    n_chunks = D_FF // FF_CHUNK

    @pl.when(pl.program_id(1) == 0)
    def _():
        carry[...] = jnp.zeros_like(carry)

    x = x_ref[0]
    h = _rms(x, g_ref[...]).astype(BF16)
    for c in range(n_chunks):
        cs = slice(c * FF_CHUNK, (c + 1) * FF_CHUNK)
        gs = slice(D_FF + c * FF_CHUNK, D_FF + (c + 1) * FF_CHUNK)
        a = _dot(h, wup_ref[:, cs])
        gt = _dot(h, wup_ref[:, gs])
        gbuf[0:SUBLANES, :] = carry[c]
        gbuf[SUBLANES:SUBLANES + TB, :] = gt
        carry[c] = gbuf[TB:TB + SUBLANES, :]
        cw = cw_ref[:, cs]
        conv = (cw[0:1] * gbuf[SUBLANES - 2:SUBLANES - 2 + TB, :]
                + cw[1:2] * gbuf[SUBLANES - 1:SUBLANES - 1 + TB, :]
                + cw[2:3] * gt + cb_ref[:, cs])
        act[:, cs] = (conv * jax.nn.sigmoid(conv) * a).astype(BF16)
    y = x + _dot(act[...], wdn_ref[...])
    if final_norm:
        y = _rms(y, fg_ref[...])
    out_ref[0] = y


def _ffn(x, g, w_up, conv_w, conv_b, w_down, final_g, final_norm):
    b, t, _ = x.shape
    tok = pl.BlockSpec((1, TB, D_MODEL), lambda i, j: (i, j, 0))
    return pl.pallas_call(
        functools.partial(_ffn_kernel, final_norm=final_norm),
        out_shape=jax.ShapeDtypeStruct(x.shape, F32),
        grid=(b, t // TB),
        in_specs=[
            tok,
            _const_spec((1, D_MODEL)),
            _const_spec((D_MODEL, 2 * D_FF)),
            _const_spec((FFN_CONV_K, D_FF)),
            _const_spec((1, D_FF)),
            _const_spec((D_FF, D_MODEL)),
            _const_spec((1, D_MODEL)),
        ],
        out_specs=tok,
        scratch_shapes=[
            pltpu.VMEM((SUBLANES + TB, FF_CHUNK), F32),
            pltpu.VMEM((D_FF // FF_CHUNK, SUBLANES, FF_CHUNK), F32),
            pltpu.VMEM((TB, D_FF), BF16),
        ],
        compiler_params=_cparams(("arbitrary", "arbitrary")),
        name="conv_ffn",
    )(x, g.reshape(1, D_MODEL), w_up, conv_w, conv_b.reshape(1, D_FF), w_down,
      final_g.reshape(1, D_MODEL))


def _pc_kernel(x_ref, g_ref, win_ref, pw_ref, ps_ref, dw_ref, db_ref, lg_ref, lb_ref, wout_ref,
               out_ref, ubuf, glbuf, shbuf, ybuf):
    tblk = pl.program_id(1)

    @pl.when(tblk == 0)
    def _():
        ubuf[0:POOL_HALO, :] = jnp.zeros((POOL_HALO, POOL_W), F32)
        glbuf[0:CONV_HALO, :] = jnp.zeros((CONV_HALO, CONV_W), F32)

    x = x_ref[0]
    h = _rms(x, g_ref[...]).astype(BF16)
    z = _dot(h, win_ref[...])
    u = z[:, :POOL_W]
    gl = z[:, POOL_W:POOL_W + CONV_W] * jax.nn.sigmoid(z[:, POOL_W + CONV_W:])
    ubuf[POOL_HALO:POOL_HALO + TB, :] = u
    glbuf[CONV_HALO:CONV_HALO + TB, :] = gl

    pos = tblk * TB + lax.broadcasted_iota(jnp.int32, (TB, 1), 0)
    for gi, w in enumerate(POOL_WINDOWS):
        ls = slice(gi * POOL_GROUP, (gi + 1) * POOL_GROUP)
        ssum = u[:, ls]
        for d in range(1, w):
            ssum = ssum + ubuf[POOL_HALO - d:POOL_HALO - d + TB, ls]
        cnt = jnp.minimum(pos + 1, w).astype(F32)
        pooled = (ssum / cnt - u[:, ls]).astype(BF16)
        ya = _dot(pooled, pw_ref[gi]) * ps_ref[:, ls]
        ybuf[:, ls] = ya.astype(BF16)

    lo = SUBLANES
    for r in range(1, SUBLANES):
        shbuf[r - 1, lo:CONV_HALO + TB, :] = glbuf[lo - r:CONV_HALO + TB - r, :]
    for c0 in range(0, TB, CONV_ROWS):
        cv = jnp.zeros((CONV_ROWS, CONV_W), F32) + db_ref[...]
        for j in range(CONV_K):
            a, r = divmod(CONV_K - 1 - j, SUBLANES)
            start = CONV_HALO - SUBLANES * a + c0
            src = glbuf if r == 0 else shbuf.at[r - 1]
            cv = cv + dw_ref[j:j + 1, :] * src[start:start + CONV_ROWS, :]
        mu = jnp.mean(cv, axis=-1, keepdims=True)
        xc = cv - mu
        yn = (xc * lax.rsqrt(jnp.mean(xc * xc, axis=-1, keepdims=True) + EPS) * lg_ref[...]
              + lb_ref[...])
        ybuf[c0:c0 + CONV_ROWS, POOL_W:] = (yn * jax.nn.sigmoid(yn)).astype(BF16)

    ubuf[0:POOL_HALO, :] = ubuf[TB:TB + POOL_HALO, :]
    glbuf[0:CONV_HALO, :] = glbuf[TB:TB + CONV_HALO, :]

    out_ref[0] = x + _dot(ybuf[...], wout_ref[...])


def _pc_mixer(x, g, w_in, pool_w, pool_scale, dw_w, dw_b, ln_g, ln_b, w_out):
    b, t, _ = x.shape
    tok = pl.BlockSpec((1, TB, D_MODEL), lambda i, j: (i, j, 0))
    return pl.pallas_call(
        _pc_kernel,
        out_shape=jax.ShapeDtypeStruct(x.shape, F32),
        grid=(b, t // TB),
        in_specs=[
            tok,
            _const_spec((1, D_MODEL)),
            _const_spec((D_MODEL, POOL_W + 2 * CONV_W)),
            _const_spec((len(POOL_WINDOWS), POOL_GROUP, POOL_GROUP)),
            _const_spec((1, POOL_W)),
            _const_spec((CONV_K, CONV_W)),
            _const_spec((1, CONV_W)),
            _const_spec((1, CONV_W)),
            _const_spec((1, CONV_W)),
            _const_spec((POOL_W + CONV_W, D_MODEL)),
        ],
        out_specs=tok,
        scratch_shapes=[
            pltpu.VMEM((POOL_HALO + TB, POOL_W), F32),
            pltpu.VMEM((CONV_HALO + TB, CONV_W), F32),
            pltpu.VMEM((SUBLANES - 1, CONV_HALO + TB, CONV_W), F32),
            pltpu.VMEM((TB, POOL_W + CONV_W), BF16),
        ],
        compiler_params=_cparams(("arbitrary", "arbitrary")),
        name="pc_mixer",
    )(x, g.reshape(1, D_MODEL), w_in, pool_w, pool_scale.reshape(1, POOL_W), dw_w,
      dw_b.reshape(1, CONV_W), ln_g.reshape(1, CONV_W), ln_b.reshape(1, CONV_W), w_out)


def _rope_tables(pos_row, inv_col, place):
    ang = inv_col * pos_row.astype(F32)
    cs = jnp.concatenate([jnp.cos(ang), jnp.sin(ang)], axis=0)
    hi = cs.astype(BF16)
    r1 = cs - hi.astype(F32)
    mid = r1.astype(BF16)
    lo = (r1 - mid.astype(F32)).astype(BF16)
    pieces = jnp.concatenate([hi, mid, lo], axis=0)
    t = lax.dot_general(pieces, place, (((0,), (0,)), ((), ())), preferred_element_type=F32)
    lane = lax.broadcasted_iota(jnp.int32, (1, HEAD_TILE), 1)
    is_rope = (lane % (HEAD_TILE // 2)) >= ROPE_LANE0
    cos_t = t[:, :HEAD_TILE] + jnp.where(is_rope, 0.0, 1.0)
    return cos_t, t[:, HEAD_TILE:]


def _rope_tile(x, cos_t, sin_t):
    return x * cos_t + pltpu.roll(x, HEAD_TILE // 2, 1) * sin_t


def _mla_proj_kernel(x_ref, pos_ref, inv_ref, place_ref, g_ref, wd_ref, qg_ref, wuq_ref, kvg_ref,
                     wuk_ref, wuv_ref, q_ref, k_ref, v_ref):
    x = x_ref[0]
    h = _rms(x, g_ref[...]).astype(BF16)
    c = _dot(h, wd_ref[...])
    cq = _rms(c[:, :Q_LORA], qg_ref[...]).astype(BF16)
    ckv = _rms(c[:, Q_LORA:Q_LORA + KV_LORA], kvg_ref[...]).astype(BF16)
    kpe = c[:, Q_LORA + KV_LORA:]

    cos_t, sin_t = _rope_tables(pos_ref[0], inv_ref[...], place_ref[...])
    kpe = _rope_tile(kpe, cos_t, sin_t)
    qscale = MLA_SCALE * LOG2E
    cos_q, sin_q = cos_t * qscale, sin_t * qscale
    q = _dot(cq, wuq_ref[...])
    kn = _dot(ckv, wuk_ref[...])
    for hd in range(MLA_HEADS):
        sl = slice(hd * HEAD_TILE, (hd + 1) * HEAD_TILE)
        q_ref[0, :, sl] = _rope_tile(q[:, sl], cos_q, sin_q).astype(BF16)
        k_ref[0, :, sl] = (kn[:, sl] + kpe).astype(BF16)
    vt = _dot_nt(wuv_ref[...], ckv)
    ones_tile = jnp.where(lax.broadcasted_iota(jnp.int32, (VT_ROWS - V_HEAD, TB), 0) == 0, 1.0, 0.0)
    for hd in range(MLA_HEADS):
        v_ref[0, hd, 0, 0:V_HEAD, :] = vt[hd * V_HEAD:(hd + 1) * V_HEAD].astype(BF16)
        v_ref[0, hd, 0, V_HEAD:VT_ROWS, :] = ones_tile.astype(BF16)


def _mla_proj(x, pos_rows, inv_col, place, g, wd, qg, wuq, kvg, wuk, wuv_t):
    b, t, _ = x.shape
    assert TB == BK
    hw = MLA_HEADS * HEAD_TILE
    dcols = Q_LORA + KV_LORA + HEAD_TILE
    tok = pl.BlockSpec((1, TB, D_MODEL), lambda i, j: (i, j, 0))
    wide = pl.BlockSpec((1, TB, hw), lambda i, j: (i, j, 0))
    return pl.pallas_call(
        _mla_proj_kernel,
        out_shape=(
            jax.ShapeDtypeStruct((b, t, hw), BF16),
            jax.ShapeDtypeStruct((b, t, hw), BF16),
            jax.ShapeDtypeStruct((b, MLA_HEADS, t // BK, VT_ROWS, BK), BF16),
        ),
        grid=(b, t // TB),
        in_specs=[
            tok,
            pl.BlockSpec((1, 1, TB), lambda i, j: (i, 0, j)),
            _const_spec(inv_col.shape),
            _const_spec(place.shape),
            _const_spec((1, D_MODEL)),
            _const_spec((D_MODEL, dcols)),
            _const_spec((1, Q_LORA)),
            _const_spec((Q_LORA, hw)),
            _const_spec((1, KV_LORA)),
            _const_spec((KV_LORA, hw)),
            _const_spec((MLA_HEADS * V_HEAD, KV_LORA)),
        ],
        out_specs=(wide, wide,
                   pl.BlockSpec((1, MLA_HEADS, 1, VT_ROWS, BK), lambda i, j: (i, 0, j, 0, 0))),
        compiler_params=_cparams(("arbitrary", "arbitrary")),
        name="mla_proj",
    )(x, pos_rows, inv_col, place, g.reshape(1, D_MODEL), wd, qg.reshape(1, Q_LORA), wuq,
      kvg.reshape(1, KV_LORA), wuk, wuv_t)


ATTN_HEADS_PER_STEP = 4


def _flash_kernel(q_ref, qn_ref, k_ref, vt_ref, o_ref, s_even, s_odd, smax0):
    qi = pl.program_id(2)
    nh = ATTN_HEADS_PER_STEP
    units = [(hd, c) for hd in range(nh) for c in range(2)]

    def scores_to(buf, j, u, queries=q_ref):
        hd, c = units[u]
        lanes = slice(hd * HEAD_TILE, (hd + 1) * HEAD_TILE)
        k0 = pl.multiple_of(j * BK, BK)
        s = _dot_nt(k_ref[0, pl.ds(k0, BK), lanes], queries[0, c * BK:(c + 1) * BK, lanes])
        buf[u] = s
        return jnp.max(s, axis=0, keepdims=True)

    @pl.when(qi == 0)
    def _():
        for u in range(len(units)):
            smax0[u] = scores_to(s_even, 0, u)

    def softmax_pv(buf, j, u, m, acc, smax, diagonal=False):
        s = buf[u]
        if diagonal:
            kv = lax.broadcasted_iota(jnp.int32, (BK, BK), 0)
            qq = lax.broadcasted_iota(jnp.int32, (BK, BK), 1)
            s = jnp.where(kv <= qq, s, NEG)
            smax = jnp.max(s, axis=0, keepdims=True)
        m_new = jnp.maximum(m, smax)
        alpha = jnp.exp2(m - m_new)
        p = jnp.exp2(s - m_new).astype(BF16)
        acc = alpha * acc + _dot(vt_ref[0, units[u][0], j], p)
        return m_new, acc

    def half_step(cur, nxt, j, carry):
        ms, accs, smax = (list(t) for t in carry)
        smax_next = []
        for u in range(len(units)):
            smax_next.append(scores_to(nxt, j + 1, u))
            ms[u], accs[u] = softmax_pv(cur, j, u, ms[u], accs[u], smax[u])
        return tuple(ms), tuple(accs), tuple(smax_next)

    def body(jj, carry):
        carry = half_step(s_even, s_odd, 2 * jj, carry)
        return half_step(s_odd, s_even, 2 * jj + 1, carry)

    init = (tuple(jnp.full((1, BK), NEG, F32) for _ in units),
            tuple(jnp.zeros((VT_ROWS, BK), F32) for _ in units),
            tuple(smax0[u] for u in range(len(units))))
    ms, accs, smax = lax.fori_loop(0, qi, body, init)

    ms, accs = list(ms), list(accs)
    for u, (hd, c) in enumerate(units):
        if c == 1:
            scores_to(s_odd, 2 * qi + 1, u)
    for u, (hd, c) in enumerate(units):
        ms[u], accs[u] = softmax_pv(s_even, 2 * qi, u, ms[u], accs[u], smax[u], diagonal=(c == 0))
        smax0[u] = scores_to(s_even, 0, u, qn_ref)
    for u, (hd, c) in enumerate(units):
        if c == 1:
            _, accs[u] = softmax_pv(s_odd, 2 * qi + 1, u, ms[u], accs[u], None, diagonal=True)
    for c in range(2):
        o_t = jnp.concatenate([accs[u][:V_HEAD] / accs[u][V_HEAD:V_HEAD + 1]
                               for u, (hd, cc) in enumerate(units) if cc == c], axis=0)
        o_ref[0, c * BK:(c + 1) * BK, :] = o_t.T.astype(BF16)


def _mla_attn(q, k, vt):
    b, t, _ = q.shape
    assert BQ == 2 * BK
    nh = ATTN_HEADS_PER_STEP
    nq = t // BQ
    return pl.pallas_call(
        _flash_kernel,
        out_shape=jax.ShapeDtypeStruct((b, t, MLA_HEADS * V_HEAD), BF16),
        grid=(b, MLA_HEADS // nh, nq),
        in_specs=[
            pl.BlockSpec((1, BQ, nh * HEAD_TILE), lambda i, hp, j: (i, j, hp)),
            pl.BlockSpec((1, BQ, nh * HEAD_TILE), lambda i, hp, j: (i, jnp.minimum(j + 1, nq - 1), hp)),
            pl.BlockSpec((1, t, nh * HEAD_TILE), lambda i, hp, j: (i, 0, hp)),
            pl.BlockSpec((1, nh, t // BK, VT_ROWS, BK), lambda i, hp, j: (i, hp, 0, 0, 0)),
        ],
        out_specs=pl.BlockSpec((1, BQ, nh * V_HEAD), lambda i, hp, j: (i, j, hp)),
        scratch_shapes=[pltpu.VMEM((2 * nh, BK, BK), F32), pltpu.VMEM((2 * nh, BK, BK), F32),
                        pltpu.VMEM((2 * nh, 1, BK), F32)],
        compiler_params=_cparams(("arbitrary", "arbitrary", "arbitrary")),
        name="mla_attn",
    )(q, q, k, vt)


def _head_tile_lanes():
    half = QK_ROPE // 2
    nope = np.concatenate([np.arange(ROPE_LANE0), HEAD_TILE // 2 + np.arange(QK_NOPE - ROPE_LANE0)])
    x1 = ROPE_LANE0 + np.arange(half)
    return np.concatenate([nope, x1, x1 + HEAD_TILE // 2])


def _to_head_tile(w):
    tile = jnp.zeros(w.shape[:-1] + (HEAD_TILE,), F32)
    return tile.at[..., _head_tile_lanes()].set(w)


def _mla_weights(w_dq_dkv, w_uq, w_ukv):
    kpe_w = w_dq_dkv[:, Q_LORA + KV_LORA:]
    kpe_tile = _to_head_tile(jnp.concatenate([jnp.zeros((D_MODEL, QK_NOPE), F32), kpe_w], axis=1))
    wd = jnp.concatenate([w_dq_dkv[:, :Q_LORA + KV_LORA], kpe_tile], axis=1)
    wuq = _to_head_tile(w_uq.reshape(Q_LORA, MLA_HEADS, QK_NOPE + QK_ROPE))
    wuq = wuq.reshape(Q_LORA, MLA_HEADS * HEAD_TILE)
    wkv = w_ukv.reshape(KV_LORA, MLA_HEADS, QK_NOPE + V_HEAD)
    wuk = _to_head_tile(jnp.pad(wkv[:, :, :QK_NOPE], ((0, 0), (0, 0), (0, QK_ROPE))))
    wuk = wuk.reshape(KV_LORA, MLA_HEADS * HEAD_TILE)
    wuv_t = wkv[:, :, QK_NOPE:].reshape(KV_LORA, MLA_HEADS * V_HEAD).T
    return tuple(w.astype(BF16) for w in (wd, wuq, wuk, wuv_t))


def _rope_constants():
    half = QK_ROPE // 2
    inv = 1.0 / (ROPE_THETA ** (np.arange(0, QK_ROPE, 2, dtype=np.float32) / QK_ROPE))
    lanes = _head_tile_lanes()
    x1, x2 = lanes[QK_NOPE:QK_NOPE + half], lanes[QK_NOPE + half:]
    place = np.zeros((2 * half, 2 * HEAD_TILE), np.float32)
    for f in range(half):
        place[f, x1[f]] = 1.0
        place[f, x2[f]] = 1.0
        place[half + f, HEAD_TILE + x1[f]] = -1.0
        place[half + f, HEAD_TILE + x2[f]] = 1.0
    return (jnp.asarray(inv.reshape(half, 1)), jnp.asarray(np.tile(place, (3, 1)), dtype=BF16))


def kernel(x, mem, positions, norm_mix_g, norm_xa_g, norm_mem_g, xa_wq, xa_wkv, xa_wo, norm_ffn_g, ffn_w_up, ffn_conv_w, ffn_conv_b, ffn_w_down, pc_w_in, pool_w, pool_scale, conv_dw_w, conv_dw_b, conv_ln_g, conv_ln_b, pc_w_out, mla_w_dq_dkv, mla_q_norm_g, mla_w_uq, mla_kv_norm_g, mla_w_ukv, mla_w_o, final_norm_g):
    depth = xa_wq.shape[0]
    b, t, _ = x.shape
    bf = lambda a: a.astype(BF16)
    mem_k, mem_v = _mem_kv(mem, norm_mem_g, bf(xa_wkv))
    xa_wq_b, xa_wo_b = bf(xa_wq), bf(xa_wo)
    w_up_b, w_down_b = bf(ffn_w_up), bf(ffn_w_down)
    pc_w_in_b, pool_w_b, pc_w_out_b = bf(pc_w_in), bf(pool_w), bf(pc_w_out)
    mla_w_o_b = bf(mla_w_o)
    pos_rows = positions.reshape(b, 1, t)
    inv_col, place = _rope_constants()

    for l in range(depth):
        if l % 2 == 0:
            e = l // 2
            x = _pc_mixer(x, norm_mix_g[l], pc_w_in_b[e], pool_w_b[e], pool_scale[e], conv_dw_w[e],
                          conv_dw_b[e], conv_ln_g[e], conv_ln_b[e], pc_w_out_b[e])
            x = _xattn(x, norm_xa_g[l], xa_wq_b[l], mem_k[l], mem_v[l], xa_wo_b[l])
        else:
            o = l // 2
            wd, wuq, wuk, wuv_t = _mla_weights(mla_w_dq_dkv[o], mla_w_uq[o], mla_w_ukv[o])
            q, k, vt = _mla_proj(x, pos_rows, inv_col, place, norm_mix_g[l], wd, mla_q_norm_g[o],
                                 wuq, mla_kv_norm_g[o], wuk, wuv_t)
            attn = _mla_attn(q, k, vt)
            x = _xattn(x, norm_xa_g[l], xa_wq_b[l], mem_k[l], mem_v[l], xa_wo_b[l],
                       attn_o=attn, wo_mla=mla_w_o_b[o])
        x = _ffn(x, norm_ffn_g[l], w_up_b[l], ffn_conv_w[l], ffn_conv_b[l], w_down_b[l],
                 final_norm_g, final_norm=(l == depth - 1))
    return x
```

```python
import functools
import math

import jax
import jax.numpy as jnp
import numpy as np
from jax import lax
from jax.experimental import pallas as pl
from jax.experimental.pallas import tpu as pltpu

D_MODEL = 1024
MEM_LEN = 256
XA_HEADS = 4
XA_HEAD_DIM = D_MODEL // XA_HEADS
POOL_W = 512
POOL_WINDOWS = (2, 4, 8, 16)
POOL_GROUP = 128
CONV_W = 512
CONV_K = 31
MLA_HEADS = 16
QK_NOPE = 64
QK_ROPE = 32
V_HEAD = 64
Q_LORA = 384
KV_LORA = 256
ROPE_THETA = 10000.0
MLA_SCALE = 1.0 / math.sqrt(QK_NOPE + QK_ROPE)
D_FF = 2816
FFN_CONV_K = 3
EPS = 1e-6
NEG = -1e30
LOG2E = 1.4426950408889634

LANES = 128
SUBLANES = 8
VMEM_LIMIT = 56 * 1024 * 1024

TB = 512
TB_FFN = 512
FF_CHUNK = 256
POOL_HALO = 16
CONV_HALO = 32
CONV_ROWS = 64
BK = 512
BQ = 2 * BK
HEAD_TILE = LANES
ROPE_LANE0 = 48
VT_ROWS = V_HEAD + 16

BF16 = jnp.bfloat16
F32 = jnp.float32


def _cparams(sem):
    return pltpu.CompilerParams(dimension_semantics=sem, vmem_limit_bytes=VMEM_LIMIT)


def _rms(x, g):
    return x * lax.rsqrt(jnp.mean(x * x, axis=-1, keepdims=True) + EPS) * g


def _dot(a, b):
    return jnp.dot(a, b, preferred_element_type=F32)


def _dot_nt(a, b):
    return lax.dot_general(a, b, (((1,), (1,)), ((), ())), preferred_element_type=F32)


def _const_spec(shape):
    n = len(shape)
    return pl.BlockSpec(shape, lambda *_: (0,) * n, pipeline_mode=pl.Buffered(1))


def _mem_kv_kernel(mem_ref, g_ref, wkv_ref, k_ref, v_ref):
    m = _rms(mem_ref[0], g_ref[0]).astype(BF16)
    kv = _dot(m, wkv_ref[0])
    k_ref[0, 0] = kv[:, :D_MODEL].astype(BF16)
    v_ref[0, 0] = kv[:, D_MODEL:].astype(BF16)


def _mem_kv(mem, norm_mem_g, xa_wkv):
    depth = xa_wkv.shape[0]
    b = mem.shape[0]
    out = jax.ShapeDtypeStruct((depth, b, MEM_LEN, D_MODEL), BF16)
    return pl.pallas_call(
        _mem_kv_kernel,
        out_shape=(out, out),
        grid=(depth, b),
        in_specs=[
            pl.BlockSpec((1, MEM_LEN, D_MODEL), lambda l, i: (i, 0, 0)),
            pl.BlockSpec((1, 1, D_MODEL), lambda l, i: (l, 0, 0)),
            pl.BlockSpec((1, D_MODEL, 2 * D_MODEL), lambda l, i: (l, 0, 0)),
        ],
        out_specs=(
            pl.BlockSpec((1, 1, MEM_LEN, D_MODEL), lambda l, i: (l, i, 0, 0)),
            pl.BlockSpec((1, 1, MEM_LEN, D_MODEL), lambda l, i: (l, i, 0, 0)),
        ),
        compiler_params=_cparams(("arbitrary", "arbitrary")),
        name="mem_kv",
    )(mem, norm_mem_g.reshape(depth, 1, D_MODEL), xa_wkv)


def _xattn_kernel(*refs, with_attn_proj):
    if with_attn_proj:
        x_ref, o_ref, wo_mla_ref, g_ref, wq_ref, k_ref, v_ref, wo_ref, out_ref = refs
        x = x_ref[0] + _dot(o_ref[0], wo_mla_ref[...])
    else:
        x_ref, g_ref, wq_ref, k_ref, v_ref, wo_ref, out_ref = refs
        x = x_ref[0]
    h = _rms(x, g_ref[...]).astype(BF16)
    q = (_dot(h, wq_ref[...]) * (XA_HEAD_DIM ** -0.5)).astype(BF16)
    heads = []
    for hd in range(XA_HEADS):
        sl = slice(hd * XA_HEAD_DIM, (hd + 1) * XA_HEAD_DIM)
        s = _dot_nt(q[:, sl], k_ref[0, :, sl])
        m = jnp.max(s, axis=-1, keepdims=True)
        p = jnp.exp(s - m)
        l = jnp.sum(p, axis=-1, keepdims=True)
        o = _dot(p.astype(BF16), v_ref[0, :, sl]) / l
        heads.append(o.astype(BF16))
    o = jnp.concatenate(heads, axis=-1)
    out_ref[0] = x + _dot(o, wo_ref[...])


def _xattn(x, g, wq, k, v, wo, attn_o=None, wo_mla=None):
    b, t, _ = x.shape
    with_attn_proj = attn_o is not None
    tok = pl.BlockSpec((1, TB, D_MODEL), lambda i, j: (i, j, 0))
    mem = pl.BlockSpec((1, MEM_LEN, D_MODEL), lambda i, j: (i, 0, 0))
    w = _const_spec((D_MODEL, D_MODEL))
    in_specs = [tok]
    args = [x]
    if with_attn_proj:
        in_specs += [tok, w]
        args += [attn_o, wo_mla]
    in_specs += [_const_spec((1, D_MODEL)), w, mem, mem, w]
    args += [g.reshape(1, D_MODEL), wq, k, v, wo]
    return pl.pallas_call(
        functools.partial(_xattn_kernel, with_attn_proj=with_attn_proj),
        out_shape=jax.ShapeDtypeStruct(x.shape, F32),
        grid=(b, t // TB),
        in_specs=in_specs,
        out_specs=tok,
        compiler_params=_cparams(("arbitrary", "arbitrary")),
        name="xattn",
    )(*args)


def _ffn_kernel(x_ref, g_ref, wup_ref, cw_ref, cb_ref, wdn_ref, fg_ref, out_ref,
                tail, act, *, final_norm, tb):
    n_chunks = D_FF // FF_CHUNK

    @pl.when(pl.program_id(1) == 0)
    def _():
        tail[...] = jnp.zeros_like(tail)

    def delayed(gt, prev, d):
        row = lax.broadcasted_iota(jnp.int32, (SUBLANES, FF_CHUNK), 0)
        rolled = pltpu.roll(gt, d, 0)
        head = jnp.where(row < d, pltpu.roll(prev, d, 0), rolled[:SUBLANES])
        return jnp.concatenate([head, rolled[SUBLANES:]], axis=0)

    x = x_ref[0]
    h = _rms(x, g_ref[...]).astype(BF16)
    for c in range(n_chunks):
        cs = slice(c * FF_CHUNK, (c + 1) * FF_CHUNK)
        gs = slice(D_FF + c * FF_CHUNK, D_FF + (c + 1) * FF_CHUNK)
        a = _dot(h, wup_ref[:, cs])
        gt = _dot(h, wup_ref[:, gs])
        prev = tail[:, cs]
        tail[:, cs] = gt[tb - SUBLANES:]
        cw = cw_ref[:, cs]
        conv = (cw[0:1] * delayed(gt, prev, 2) + cw[1:2] * delayed(gt, prev, 1)
                + cw[2:3] * gt + cb_ref[:, cs])
        act[:, cs] = (conv * jax.nn.sigmoid(conv) * a).astype(BF16)
    y = x + _dot(act[...], wdn_ref[...])
    if final_norm:
        y = _rms(y, fg_ref[...])
    out_ref[0] = y


def _ffn(x, g, w_up, conv_w, conv_b, w_down, final_g, final_norm):
    b, t, _ = x.shape
    tb = TB_FFN
    tok = pl.BlockSpec((1, tb, D_MODEL), lambda i, j: (i, j, 0))
    return pl.pallas_call(
        functools.partial(_ffn_kernel, final_norm=final_norm, tb=tb),
        out_shape=jax.ShapeDtypeStruct(x.shape, F32),
        grid=(b, t // tb),
        in_specs=[
            tok,
            _const_spec((1, D_MODEL)),
            _const_spec((D_MODEL, 2 * D_FF)),
            _const_spec((FFN_CONV_K, D_FF)),
            _const_spec((1, D_FF)),
            _const_spec((D_FF, D_MODEL)),
            _const_spec((1, D_MODEL)),
        ],
        out_specs=tok,
        scratch_shapes=[
            pltpu.VMEM((SUBLANES, D_FF), F32),
            pltpu.VMEM((tb, D_FF), BF16),
        ],
        compiler_params=_cparams(("arbitrary", "arbitrary")),
        name="conv_ffn",
    )(x, g.reshape(1, D_MODEL), w_up, conv_w, conv_b.reshape(1, D_FF), w_down,
      final_g.reshape(1, D_MODEL))


def _pc_kernel(x_ref, g_ref, win_ref, pw_ref, ps_ref, dw_ref, db_ref, lg_ref, lb_ref, wout_ref,
               out_ref, ubuf, glbuf, shbuf, ybuf):
    tblk = pl.program_id(1)

    @pl.when(tblk == 0)
    def _():
        ubuf[0:POOL_HALO, :] = jnp.zeros((POOL_HALO, POOL_W), F32)
        glbuf[0:CONV_HALO, :] = jnp.zeros((CONV_HALO, CONV_W), F32)

    x = x_ref[0]
    h = _rms(x, g_ref[...]).astype(BF16)
    z = _dot(h, win_ref[...])
    u = z[:, :POOL_W]
    gl = z[:, POOL_W:POOL_W + CONV_W] * jax.nn.sigmoid(z[:, POOL_W + CONV_W:])
    ubuf[POOL_HALO:POOL_HALO + TB, :] = u
    glbuf[CONV_HALO:CONV_HALO + TB, :] = gl

    pos = tblk * TB + lax.broadcasted_iota(jnp.int32, (TB, 1), 0)
    for gi, w in enumerate(POOL_WINDOWS):
        ls = slice(gi * POOL_GROUP, (gi + 1) * POOL_GROUP)
        ue = ubuf[:, ls]
        s8 = ue
        for d in range(1, min(w, SUBLANES)):
            s8 = s8 + pltpu.roll(ue, d, 0)
        ssum = s8[POOL_HALO:]
        if w > SUBLANES:
            assert w == 2 * SUBLANES
            ssum = ssum + s8[POOL_HALO - SUBLANES:POOL_HALO - SUBLANES + TB]
        cnt = jnp.minimum(pos + 1, w).astype(F32)
        pooled = (ssum / cnt - u[:, ls]).astype(BF16)
        ya = _dot(pooled, pw_ref[gi]) * ps_ref[:, ls]
        ybuf[:, ls] = ya.astype(BF16)

    lo = SUBLANES
    gle = glbuf[...]
    for r in range(1, SUBLANES):
        shbuf[r - 1, lo:CONV_HALO + TB, :] = pltpu.roll(gle, r, 0)[lo:]
    tiles = CONV_ROWS // SUBLANES
    for c0 in range(0, TB, CONV_ROWS):
        cv = jnp.zeros((tiles, SUBLANES, CONV_W), F32) + db_ref[...]
        for j in range(CONV_K):
            a, r = divmod(CONV_K - 1 - j, SUBLANES)
            start = CONV_HALO - SUBLANES * a + c0
            src = glbuf if r == 0 else shbuf.at[r - 1]
            win = src[start:start + CONV_ROWS, :].reshape(tiles, SUBLANES, CONV_W)
            cv = cv + dw_ref[j] * win
        mu = jnp.mean(cv, axis=-1, keepdims=True)
        xc = cv - mu
        yn = (xc * lax.rsqrt(jnp.mean(xc * xc, axis=-1, keepdims=True) + EPS) * lg_ref[...]
              + lb_ref[...])
        act = (yn * jax.nn.sigmoid(yn)).reshape(CONV_ROWS, CONV_W)
        ybuf[c0:c0 + CONV_ROWS, POOL_W:] = act.astype(BF16)

    ubuf[0:POOL_HALO, :] = ubuf[TB:TB + POOL_HALO, :]
    glbuf[0:CONV_HALO, :] = glbuf[TB:TB + CONV_HALO, :]

    out_ref[0] = x + _dot(ybuf[...], wout_ref[...])


def _pc_mixer(x, g, w_in, pool_w, pool_scale, dw_w, dw_b, ln_g, ln_b, w_out):
    b, t, _ = x.shape
    tok = pl.BlockSpec((1, TB, D_MODEL), lambda i, j: (i, j, 0))
    return pl.pallas_call(
        _pc_kernel,
        out_shape=jax.ShapeDtypeStruct(x.shape, F32),
        grid=(b, t // TB),
        in_specs=[
            tok,
            _const_spec((1, D_MODEL)),
            _const_spec((D_MODEL, POOL_W + 2 * CONV_W)),
            _const_spec((len(POOL_WINDOWS), POOL_GROUP, POOL_GROUP)),
            _const_spec((1, POOL_W)),
            _const_spec((CONV_K, SUBLANES, CONV_W)),
            _const_spec((1, CONV_W)),
            _const_spec((1, CONV_W)),
            _const_spec((1, CONV_W)),
            _const_spec((POOL_W + CONV_W, D_MODEL)),
        ],
        out_specs=tok,
        scratch_shapes=[
            pltpu.VMEM((POOL_HALO + TB, POOL_W), F32),
            pltpu.VMEM((CONV_HALO + TB, CONV_W), F32),
            pltpu.VMEM((SUBLANES - 1, CONV_HALO + TB, CONV_W), F32),
            pltpu.VMEM((TB, POOL_W + CONV_W), BF16),
        ],
        compiler_params=_cparams(("arbitrary", "arbitrary")),
        name="pc_mixer",
    )(x, g.reshape(1, D_MODEL), w_in, pool_w, pool_scale.reshape(1, POOL_W),
      jnp.broadcast_to(dw_w[:, None, :], (CONV_K, SUBLANES, CONV_W)),
      dw_b.reshape(1, CONV_W), ln_g.reshape(1, CONV_W), ln_b.reshape(1, CONV_W), w_out)


def _rope_tables(pos_row, inv_col, place):
    ang = inv_col * pos_row.astype(F32)
    cs = jnp.concatenate([jnp.cos(ang), jnp.sin(ang)], axis=0)
    hi = cs.astype(BF16)
    r1 = cs - hi.astype(F32)
    mid = r1.astype(BF16)
    lo = (r1 - mid.astype(F32)).astype(BF16)
    pieces = jnp.concatenate([hi, mid, lo], axis=0)
    t = lax.dot_general(pieces, place, (((0,), (0,)), ((), ())), preferred_element_type=F32)
    lane = lax.broadcasted_iota(jnp.int32, (1, HEAD_TILE), 1)
    is_rope = (lane % (HEAD_TILE // 2)) >= ROPE_LANE0
    cos_t = t[:, :HEAD_TILE] + jnp.where(is_rope, 0.0, 1.0)
    return cos_t, t[:, HEAD_TILE:]


def _rope_tile(x, cos_t, sin_t):
    return x * cos_t + pltpu.roll(x, HEAD_TILE // 2, 1) * sin_t


def _mla_proj_kernel(x_ref, pos_ref, inv_ref, place_ref, g_ref, wd_ref, qg_ref, wuq_ref, kvg_ref,
                     wuk_ref, wuv_ref, q_ref, k_ref, v_ref):
    x = x_ref[0]
    h = _rms(x, g_ref[...]).astype(BF16)
    c = _dot(h, wd_ref[...])
    cq = _rms(c[:, :Q_LORA], qg_ref[...]).astype(BF16)
    ckv = _rms(c[:, Q_LORA:Q_LORA + KV_LORA], kvg_ref[...]).astype(BF16)
    kpe = c[:, Q_LORA + KV_LORA:]

    cos_t, sin_t = _rope_tables(pos_ref[0], inv_ref[...], place_ref[...])
    kpe = _rope_tile(kpe, cos_t, sin_t)
    qscale = MLA_SCALE * LOG2E
    cos_q, sin_q = cos_t * qscale, sin_t * qscale
    q = _dot(cq, wuq_ref[...])
    kn = _dot(ckv, wuk_ref[...])
    for hd in range(MLA_HEADS):
        sl = slice(hd * HEAD_TILE, (hd + 1) * HEAD_TILE)
        q_ref[0, :, sl] = _rope_tile(q[:, sl], cos_q, sin_q).astype(BF16)
        k_ref[0, :, sl] = (kn[:, sl] + kpe).astype(BF16)
    vt = _dot_nt(wuv_ref[...], ckv)
    ones_tile = jnp.where(lax.broadcasted_iota(jnp.int32, (VT_ROWS - V_HEAD, TB), 0) == 0, 1.0, 0.0)
    for hd in range(MLA_HEADS):
        v_ref[0, hd, 0, 0:V_HEAD, :] = vt[hd * V_HEAD:(hd + 1) * V_HEAD].astype(BF16)
        v_ref[0, hd, 0, V_HEAD:VT_ROWS, :] = ones_tile.astype(BF16)


def _mla_proj(x, pos_rows, inv_col, place, g, wd, qg, wuq, kvg, wuk, wuv_t):
    b, t, _ = x.shape
    assert TB == BK
    hw = MLA_HEADS * HEAD_TILE
    dcols = Q_LORA + KV_LORA + HEAD_TILE
    tok = pl.BlockSpec((1, TB, D_MODEL), lambda i, j: (i, j, 0))
    wide = pl.BlockSpec((1, TB, hw), lambda i, j: (i, j, 0))
    return pl.pallas_call(
        _mla_proj_kernel,
        out_shape=(
            jax.ShapeDtypeStruct((b, t, hw), BF16),
            jax.ShapeDtypeStruct((b, t, hw), BF16),
            jax.ShapeDtypeStruct((b, MLA_HEADS, t // BK, VT_ROWS, BK), BF16),
        ),
        grid=(b, t // TB),
        in_specs=[
            tok,
            pl.BlockSpec((1, 1, TB), lambda i, j: (i, 0, j)),
            _const_spec(inv_col.shape),
            _const_spec(place.shape),
            _const_spec((1, D_MODEL)),
            _const_spec((D_MODEL, dcols)),
            _const_spec((1, Q_LORA)),
            _const_spec((Q_LORA, hw)),
            _const_spec((1, KV_LORA)),
            _const_spec((KV_LORA, hw)),
            _const_spec((MLA_HEADS * V_HEAD, KV_LORA)),
        ],
        out_specs=(wide, wide,
                   pl.BlockSpec((1, MLA_HEADS, 1, VT_ROWS, BK), lambda i, j: (i, 0, j, 0, 0))),
        compiler_params=_cparams(("arbitrary", "arbitrary")),
        name="mla_proj",
    )(x, pos_rows, inv_col, place, g.reshape(1, D_MODEL), wd, qg.reshape(1, Q_LORA), wuq,
      kvg.reshape(1, KV_LORA), wuk, wuv_t)


ATTN_HEADS_PER_STEP = 4


def _flash_kernel(q_ref, qn_ref, k_ref, vt_ref, o_ref, s_even, s_odd, smax0):
    qi = pl.program_id(2)
    nh = ATTN_HEADS_PER_STEP
    units = [(hd, c) for hd in range(nh) for c in range(2)]

    def scores_to(buf, j, u, queries=q_ref):
        hd, c = units[u]
        lanes = slice(hd * HEAD_TILE, (hd + 1) * HEAD_TILE)
        k0 = pl.multiple_of(j * BK, BK)
        s = _dot_nt(k_ref[0, pl.ds(k0, BK), lanes], queries[0, c * BK:(c + 1) * BK, lanes])
        buf[u] = s
        return jnp.max(s, axis=0, keepdims=True)

    @pl.when(qi == 0)
    def _():
        for u in range(len(units)):
            smax0[u] = scores_to(s_even, 0, u)

    def softmax_pv(buf, j, u, m, acc, smax, diagonal=False):
        s = buf[u]
        if diagonal:
            kv = lax.broadcasted_iota(jnp.int32, (BK, BK), 0)
            qq = lax.broadcasted_iota(jnp.int32, (BK, BK), 1)
            s = jnp.where(kv <= qq, s, NEG)
            smax = jnp.max(s, axis=0, keepdims=True)
        m_new = jnp.maximum(m, smax)
        alpha = jnp.exp2(m - m_new)
        p = jnp.exp2(s - m_new).astype(BF16)
        acc = alpha * acc + _dot(vt_ref[0, units[u][0], j], p)
        return m_new, acc

    def half_step(cur, nxt, j, carry):
        ms, accs, smax = (list(t) for t in carry)
        smax_next = []
        for u in range(len(units)):
            smax_next.append(scores_to(nxt, j + 1, u))
            ms[u], accs[u] = softmax_pv(cur, j, u, ms[u], accs[u], smax[u])
        return tuple(ms), tuple(accs), tuple(smax_next)

    def body(jj, carry):
        carry = half_step(s_even, s_odd, 2 * jj, carry)
        return half_step(s_odd, s_even, 2 * jj + 1, carry)

    init = (tuple(jnp.full((1, BK), NEG, F32) for _ in units),
            tuple(jnp.zeros((VT_ROWS, BK), F32) for _ in units),
            tuple(smax0[u] for u in range(len(units))))
    ms, accs, smax = lax.fori_loop(0, qi, body, init)

    ms, accs = list(ms), list(accs)
    for u, (hd, c) in enumerate(units):
        if c == 1:
            scores_to(s_odd, 2 * qi + 1, u)
    for u, (hd, c) in enumerate(units):
        ms[u], accs[u] = softmax_pv(s_even, 2 * qi, u, ms[u], accs[u], smax[u], diagonal=(c == 0))
        smax0[u] = scores_to(s_even, 0, u, qn_ref)
    for u, (hd, c) in enumerate(units):
        if c == 1:
            _, accs[u] = softmax_pv(s_odd, 2 * qi + 1, u, ms[u], accs[u], None, diagonal=True)
    for c in range(2):
        o_t = jnp.concatenate([accs[u][:V_HEAD] / accs[u][V_HEAD:V_HEAD + 1]
                               for u, (hd, cc) in enumerate(units) if cc == c], axis=0)
        o_ref[0, c * BK:(c + 1) * BK, :] = o_t.T.astype(BF16)


def _mla_attn(q, k, vt):
    b, t, _ = q.shape
    assert BQ == 2 * BK
    nh = ATTN_HEADS_PER_STEP
    nq = t // BQ
    return pl.pallas_call(
        _flash_kernel,
        out_shape=jax.ShapeDtypeStruct((b, t, MLA_HEADS * V_HEAD), BF16),
        grid=(b, MLA_HEADS // nh, nq),
        in_specs=[
            pl.BlockSpec((1, BQ, nh * HEAD_TILE), lambda i, hp, j: (i, j, hp)),
            pl.BlockSpec((1, BQ, nh * HEAD_TILE), lambda i, hp, j: (i, jnp.minimum(j + 1, nq - 1), hp)),
            pl.BlockSpec((1, t, nh * HEAD_TILE), lambda i, hp, j: (i, 0, hp)),
            pl.BlockSpec((1, nh, t // BK, VT_ROWS, BK), lambda i, hp, j: (i, hp, 0, 0, 0)),
        ],
        out_specs=pl.BlockSpec((1, BQ, nh * V_HEAD), lambda i, hp, j: (i, j, hp)),
        scratch_shapes=[pltpu.VMEM((2 * nh, BK, BK), F32), pltpu.VMEM((2 * nh, BK, BK), F32),
                        pltpu.VMEM((2 * nh, 1, BK), F32)],
        compiler_params=_cparams(("arbitrary", "arbitrary", "arbitrary")),
        name="mla_attn",
    )(q, q, k, vt)


def _head_tile_lanes():
    half = QK_ROPE // 2
    nope = np.concatenate([np.arange(ROPE_LANE0), HEAD_TILE // 2 + np.arange(QK_NOPE - ROPE_LANE0)])
    x1 = ROPE_LANE0 + np.arange(half)
    return np.concatenate([nope, x1, x1 + HEAD_TILE // 2])


def _to_head_tile(w):
    tile = jnp.zeros(w.shape[:-1] + (HEAD_TILE,), F32)
    return tile.at[..., _head_tile_lanes()].set(w)


def _mla_weights(w_dq_dkv, w_uq, w_ukv):
    kpe_w = w_dq_dkv[:, Q_LORA + KV_LORA:]
    kpe_tile = _to_head_tile(jnp.concatenate([jnp.zeros((D_MODEL, QK_NOPE), F32), kpe_w], axis=1))
    wd = jnp.concatenate([w_dq_dkv[:, :Q_LORA + KV_LORA], kpe_tile], axis=1)
    wuq = _to_head_tile(w_uq.reshape(Q_LORA, MLA_HEADS, QK_NOPE + QK_ROPE))
    wuq = wuq.reshape(Q_LORA, MLA_HEADS * HEAD_TILE)
    wkv = w_ukv.reshape(KV_LORA, MLA_HEADS, QK_NOPE + V_HEAD)
    wuk = _to_head_tile(jnp.pad(wkv[:, :, :QK_NOPE], ((0, 0), (0, 0), (0, QK_ROPE))))
    wuk = wuk.reshape(KV_LORA, MLA_HEADS * HEAD_TILE)
    wuv_t = wkv[:, :, QK_NOPE:].reshape(KV_LORA, MLA_HEADS * V_HEAD).T
    return tuple(w.astype(BF16) for w in (wd, wuq, wuk, wuv_t))


def _rope_constants():
    half = QK_ROPE // 2
    inv = 1.0 / (ROPE_THETA ** (np.arange(0, QK_ROPE, 2, dtype=np.float32) / QK_ROPE))
    lanes = _head_tile_lanes()
    x1, x2 = lanes[QK_NOPE:QK_NOPE + half], lanes[QK_NOPE + half:]
    place = np.zeros((2 * half, 2 * HEAD_TILE), np.float32)
    for f in range(half):
        place[f, x1[f]] = 1.0
        place[f, x2[f]] = 1.0
        place[half + f, HEAD_TILE + x1[f]] = -1.0
        place[half + f, HEAD_TILE + x2[f]] = 1.0
    return (jnp.asarray(inv.reshape(half, 1)), jnp.asarray(np.tile(place, (3, 1)), dtype=BF16))


def kernel(x, mem, positions, norm_mix_g, norm_xa_g, norm_mem_g, xa_wq, xa_wkv, xa_wo, norm_ffn_g, ffn_w_up, ffn_conv_w, ffn_conv_b, ffn_w_down, pc_w_in, pool_w, pool_scale, conv_dw_w, conv_dw_b, conv_ln_g, conv_ln_b, pc_w_out, mla_w_dq_dkv, mla_q_norm_g, mla_w_uq, mla_kv_norm_g, mla_w_ukv, mla_w_o, final_norm_g):
    depth = xa_wq.shape[0]
    b, t, _ = x.shape
    bf = lambda a: a.astype(BF16)
    mem_k, mem_v = _mem_kv(mem, norm_mem_g, bf(xa_wkv))
    xa_wq_b, xa_wo_b = bf(xa_wq), bf(xa_wo)
    w_up_b, w_down_b = bf(ffn_w_up), bf(ffn_w_down)
    pc_w_in_b, pool_w_b, pc_w_out_b = bf(pc_w_in), bf(pool_w), bf(pc_w_out)
    mla_w_o_b = bf(mla_w_o)
    pos_rows = positions.reshape(b, 1, t)
    inv_col, place = _rope_constants()

    for l in range(depth):
        if l % 2 == 0:
            e = l // 2
            x = _pc_mixer(x, norm_mix_g[l], pc_w_in_b[e], pool_w_b[e], pool_scale[e], conv_dw_w[e],
                          conv_dw_b[e], conv_ln_g[e], conv_ln_b[e], pc_w_out_b[e])
            x = _xattn(x, norm_xa_g[l], xa_wq_b[l], mem_k[l], mem_v[l], xa_wo_b[l])
        else:
            o = l // 2
            wd, wuq, wuk, wuv_t = _mla_weights(mla_w_dq_dkv[o], mla_w_uq[o], mla_w_ukv[o])
            q, k, vt = _mla_proj(x, pos_rows, inv_col, place, norm_mix_g[l], wd, mla_q_norm_g[o],
                                 wuq, mla_kv_norm_g[o], wuk, wuv_t)
            attn = _mla_attn(q, k, vt)
            x = _xattn(x, norm_xa_g[l], xa_wq_b[l], mem_k[l], mem_v[l], xa_wo_b[l],
                       attn_o=attn, wo_mla=mla_w_o_b[o])
        x = _ffn(x, norm_ffn_g[l], w_up_b[l], ffn_conv_w[l], ffn_conv_b[l], w_down_b[l],
                 final_norm_g, final_norm=(l == depth - 1))
    return x
```

```python
import functools
import math

import jax
import jax.numpy as jnp
import numpy as np
from jax import lax
from jax.experimental import pallas as pl
from jax.experimental.pallas import tpu as pltpu

D_MODEL = 1024
MEM_LEN = 256
XA_HEADS = 4
XA_HEAD_DIM = D_MODEL // XA_HEADS
POOL_W = 512
POOL_WINDOWS = (2, 4, 8, 16)
POOL_GROUP = 128
CONV_W = 512
CONV_K = 31
MLA_HEADS = 16
QK_NOPE = 64
QK_ROPE = 32
V_HEAD = 64
Q_LORA = 384
KV_LORA = 256
ROPE_THETA = 10000.0
MLA_SCALE = 1.0 / math.sqrt(QK_NOPE + QK_ROPE)
D_FF = 2816
FFN_CONV_K = 3
EPS = 1e-6
NEG = -1e30
LOG2E = 1.4426950408889634

LANES = 128
SUBLANES = 8
VMEM_LIMIT = 56 * 1024 * 1024

TB = 512
TB_FFN = 512
TB_XA = 1024
TB_PROJ = 1024
FF_CHUNK = 256
POOL_HALO = 16
CONV_HALO = 32
CONV_ROWS = 64
BK = 512
BQ = 2 * BK
HEAD_TILE = LANES
ROPE_LANE0 = 48
VT_ROWS = V_HEAD + 16

BF16 = jnp.bfloat16
F32 = jnp.float32


def _cparams(sem):
    return pltpu.CompilerParams(dimension_semantics=sem, vmem_limit_bytes=VMEM_LIMIT)


def _rms(x, g):
    return x * lax.rsqrt(jnp.mean(x * x, axis=-1, keepdims=True) + EPS) * g


def _dot(a, b):
    return jnp.dot(a, b, preferred_element_type=F32)


def _dot_nt(a, b):
    return lax.dot_general(a, b, (((1,), (1,)), ((), ())), preferred_element_type=F32)


def _const_spec(shape):
    n = len(shape)
    return pl.BlockSpec(shape, lambda *_: (0,) * n, pipeline_mode=pl.Buffered(1))


def _mem_kv_kernel(mem_ref, g_ref, wkv_ref, k_ref, v_ref):
    m = _rms(mem_ref[0], g_ref[0]).astype(BF16)
    kv = _dot(m, wkv_ref[0])
    k_ref[0, 0] = kv[:, :D_MODEL].astype(BF16)
    v_ref[0, 0] = kv[:, D_MODEL:].astype(BF16)


def _mem_kv(mem, norm_mem_g, xa_wkv):
    depth = xa_wkv.shape[0]
    b = mem.shape[0]
    out = jax.ShapeDtypeStruct((depth, b, MEM_LEN, D_MODEL), BF16)
    return pl.pallas_call(
        _mem_kv_kernel,
        out_shape=(out, out),
        grid=(depth, b),
        in_specs=[
            pl.BlockSpec((1, MEM_LEN, D_MODEL), lambda l, i: (i, 0, 0)),
            pl.BlockSpec((1, 1, D_MODEL), lambda l, i: (l, 0, 0)),
            pl.BlockSpec((1, D_MODEL, 2 * D_MODEL), lambda l, i: (l, 0, 0)),
        ],
        out_specs=(
            pl.BlockSpec((1, 1, MEM_LEN, D_MODEL), lambda l, i: (l, i, 0, 0)),
            pl.BlockSpec((1, 1, MEM_LEN, D_MODEL), lambda l, i: (l, i, 0, 0)),
        ),
        compiler_params=_cparams(("arbitrary", "arbitrary")),
        name="mem_kv",
    )(mem, norm_mem_g.reshape(depth, 1, D_MODEL), xa_wkv)


def _xattn_kernel(*refs, with_attn_proj):
    if with_attn_proj:
        x_ref, o_ref, wo_mla_ref, g_ref, wq_ref, k_ref, v_ref, wo_ref, out_ref = refs
        x = x_ref[0] + _dot(o_ref[0], wo_mla_ref[...])
    else:
        x_ref, g_ref, wq_ref, k_ref, v_ref, wo_ref, out_ref = refs
        x = x_ref[0]
    h = _rms(x, g_ref[...]).astype(BF16)
    q = (_dot(h, wq_ref[...]) * (XA_HEAD_DIM ** -0.5)).astype(BF16)
    heads = []
    for hd in range(XA_HEADS):
        sl = slice(hd * XA_HEAD_DIM, (hd + 1) * XA_HEAD_DIM)
        s = _dot_nt(q[:, sl], k_ref[0, :, sl])
        m = jnp.max(s, axis=-1, keepdims=True)
        p = jnp.exp(s - m)
        l = jnp.sum(p, axis=-1, keepdims=True)
        o = _dot(p.astype(BF16), v_ref[0, :, sl]) / l
        heads.append(o.astype(BF16))
    o = jnp.concatenate(heads, axis=-1)
    out_ref[0] = x + _dot(o, wo_ref[...])


def _xattn(x, g, wq, k, v, wo, attn_o=None, wo_mla=None):
    b, t, _ = x.shape
    with_attn_proj = attn_o is not None
    tok = pl.BlockSpec((1, TB_XA, D_MODEL), lambda i, j: (i, j, 0))
    mem = pl.BlockSpec((1, MEM_LEN, D_MODEL), lambda i, j: (i, 0, 0))
    w = _const_spec((D_MODEL, D_MODEL))
    in_specs = [tok]
    args = [x]
    if with_attn_proj:
        in_specs += [tok, w]
        args += [attn_o, wo_mla]
    in_specs += [_const_spec((1, D_MODEL)), w, mem, mem, w]
    args += [g.reshape(1, D_MODEL), wq, k, v, wo]
    return pl.pallas_call(
        functools.partial(_xattn_kernel, with_attn_proj=with_attn_proj),
        out_shape=jax.ShapeDtypeStruct(x.shape, F32),
        grid=(b, t // TB_XA),
        in_specs=in_specs,
        out_specs=tok,
        compiler_params=_cparams(("arbitrary", "arbitrary")),
        name="xattn",
    )(*args)


def _ffn_kernel(x_ref, g_ref, wup_ref, cw_ref, cb_ref, wdn_ref, fg_ref, out_ref,
                tail, act, *, final_norm, tb):
    n_chunks = D_FF // FF_CHUNK

    @pl.when(pl.program_id(1) == 0)
    def _():
        tail[...] = jnp.zeros_like(tail)

    def delayed(gt, prev, d):
        row = lax.broadcasted_iota(jnp.int32, (SUBLANES, FF_CHUNK), 0)
        rolled = pltpu.roll(gt, d, 0)
        head = jnp.where(row < d, pltpu.roll(prev, d, 0), rolled[:SUBLANES])
        return jnp.concatenate([head, rolled[SUBLANES:]], axis=0)

    x = x_ref[0]
    h = _rms(x, g_ref[...]).astype(BF16)
    for c in range(n_chunks):
        cs = slice(c * FF_CHUNK, (c + 1) * FF_CHUNK)
        gs = slice(D_FF + c * FF_CHUNK, D_FF + (c + 1) * FF_CHUNK)
        a = _dot(h, wup_ref[:, cs])
        gt = _dot(h, wup_ref[:, gs])
        prev = tail[:, cs]
        tail[:, cs] = gt[tb - SUBLANES:]
        cw = cw_ref[:, cs]
        conv = (cw[0:1] * delayed(gt, prev, 2) + cw[1:2] * delayed(gt, prev, 1)
                + cw[2:3] * gt + cb_ref[:, cs])
        act[:, cs] = (conv * jax.nn.sigmoid(conv) * a).astype(BF16)
    y = x + _dot(act[...], wdn_ref[...])
    if final_norm:
        y = _rms(y, fg_ref[...])
    out_ref[0] = y


def _ffn(x, g, w_up, conv_w, conv_b, w_down, final_g, final_norm):
    b, t, _ = x.shape
    tb = TB_FFN
    tok = pl.BlockSpec((1, tb, D_MODEL), lambda i, j: (i, j, 0))
    return pl.pallas_call(
        functools.partial(_ffn_kernel, final_norm=final_norm, tb=tb),
        out_shape=jax.ShapeDtypeStruct(x.shape, F32),
        grid=(b, t // tb),
        in_specs=[
            tok,
            _const_spec((1, D_MODEL)),
            _const_spec((D_MODEL, 2 * D_FF)),
            _const_spec((FFN_CONV_K, D_FF)),
            _const_spec((1, D_FF)),
            _const_spec((D_FF, D_MODEL)),
            _const_spec((1, D_MODEL)),
        ],
        out_specs=tok,
        scratch_shapes=[
            pltpu.VMEM((SUBLANES, D_FF), F32),
            pltpu.VMEM((tb, D_FF), BF16),
        ],
        compiler_params=_cparams(("arbitrary", "arbitrary")),
        name="conv_ffn",
    )(x, g.reshape(1, D_MODEL), w_up, conv_w, conv_b.reshape(1, D_FF), w_down,
      final_g.reshape(1, D_MODEL))


def _pc_kernel(x_ref, g_ref, win_ref, pw_ref, ps_ref, dw_ref, db_ref, lg_ref, lb_ref, wout_ref,
               out_ref, ubuf, glbuf, shbuf, ybuf):
    tblk = pl.program_id(1)

    @pl.when(tblk == 0)
    def _():
        ubuf[0:POOL_HALO, :] = jnp.zeros((POOL_HALO, POOL_W), F32)
        glbuf[0:CONV_HALO, :] = jnp.zeros((CONV_HALO, CONV_W), F32)

    x = x_ref[0]
    h = _rms(x, g_ref[...]).astype(BF16)
    z = _dot(h, win_ref[...])
    u = z[:, :POOL_W]
    gl = z[:, POOL_W:POOL_W + CONV_W] * jax.nn.sigmoid(z[:, POOL_W + CONV_W:])
    ubuf[POOL_HALO:POOL_HALO + TB, :] = u
    glbuf[CONV_HALO:CONV_HALO + TB, :] = gl

    pos = tblk * TB + lax.broadcasted_iota(jnp.int32, (TB, 1), 0)
    for gi, w in enumerate(POOL_WINDOWS):
        ls = slice(gi * POOL_GROUP, (gi + 1) * POOL_GROUP)
        ue = ubuf[:, ls]
        s8 = ue
        for d in range(1, min(w, SUBLANES)):
            s8 = s8 + pltpu.roll(ue, d, 0)
        ssum = s8[POOL_HALO:]
        if w > SUBLANES:
            assert w == 2 * SUBLANES
            ssum = ssum + s8[POOL_HALO - SUBLANES:POOL_HALO - SUBLANES + TB]
        cnt = jnp.minimum(pos + 1, w).astype(F32)
        pooled = (ssum / cnt - u[:, ls]).astype(BF16)
        ya = _dot(pooled, pw_ref[gi]) * ps_ref[:, ls]
        ybuf[:, ls] = ya.astype(BF16)

    lo = SUBLANES
    gle = glbuf[...]
    for r in range(1, SUBLANES):
        shbuf[r - 1, lo:CONV_HALO + TB, :] = pltpu.roll(gle, r, 0)[lo:]
    tiles = CONV_ROWS // SUBLANES
    for c0 in range(0, TB, CONV_ROWS):
        cv = jnp.zeros((tiles, SUBLANES, CONV_W), F32) + db_ref[...]
        for j in range(CONV_K):
            a, r = divmod(CONV_K - 1 - j, SUBLANES)
            start = CONV_HALO - SUBLANES * a + c0
            src = glbuf if r == 0 else shbuf.at[r - 1]
            win = src[start:start + CONV_ROWS, :].reshape(tiles, SUBLANES, CONV_W)
            cv = cv + dw_ref[j] * win
        mu = jnp.mean(cv, axis=-1, keepdims=True)
        xc = cv - mu
        yn = (xc * lax.rsqrt(jnp.mean(xc * xc, axis=-1, keepdims=True) + EPS) * lg_ref[...]
              + lb_ref[...])
        act = (yn * jax.nn.sigmoid(yn)).reshape(CONV_ROWS, CONV_W)
        ybuf[c0:c0 + CONV_ROWS, POOL_W:] = act.astype(BF16)

    ubuf[0:POOL_HALO, :] = ubuf[TB:TB + POOL_HALO, :]
    glbuf[0:CONV_HALO, :] = glbuf[TB:TB + CONV_HALO, :]

    out_ref[0] = x + _dot(ybuf[...], wout_ref[...])


def _pc_mixer(x, g, w_in, pool_w, pool_scale, dw_w, dw_b, ln_g, ln_b, w_out):
    b, t, _ = x.shape
    tok = pl.BlockSpec((1, TB, D_MODEL), lambda i, j: (i, j, 0))
    return pl.pallas_call(
        _pc_kernel,
        out_shape=jax.ShapeDtypeStruct(x.shape, F32),
        grid=(b, t // TB),
        in_specs=[
            tok,
            _const_spec((1, D_MODEL)),
            _const_spec((D_MODEL, POOL_W + 2 * CONV_W)),
            _const_spec((len(POOL_WINDOWS), POOL_GROUP, POOL_GROUP)),
            _const_spec((1, POOL_W)),
            _const_spec((CONV_K, SUBLANES, CONV_W)),
            _const_spec((1, CONV_W)),
            _const_spec((1, CONV_W)),
            _const_spec((1, CONV_W)),
            _const_spec((POOL_W + CONV_W, D_MODEL)),
        ],
        out_specs=tok,
        scratch_shapes=[
            pltpu.VMEM((POOL_HALO + TB, POOL_W), F32),
            pltpu.VMEM((CONV_HALO + TB, CONV_W), F32),
            pltpu.VMEM((SUBLANES - 1, CONV_HALO + TB, CONV_W), F32),
            pltpu.VMEM((TB, POOL_W + CONV_W), BF16),
        ],
        compiler_params=_cparams(("arbitrary", "arbitrary")),
        name="pc_mixer",
    )(x, g.reshape(1, D_MODEL), w_in, pool_w, pool_scale.reshape(1, POOL_W),
      jnp.broadcast_to(dw_w[:, None, :], (CONV_K, SUBLANES, CONV_W)),
      dw_b.reshape(1, CONV_W), ln_g.reshape(1, CONV_W), ln_b.reshape(1, CONV_W), w_out)


def _rope_tables(pos_row, inv_col, place):
    ang = inv_col * pos_row.astype(F32)
    cs = jnp.concatenate([jnp.cos(ang), jnp.sin(ang)], axis=0)
    hi = cs.astype(BF16)
    r1 = cs - hi.astype(F32)
    mid = r1.astype(BF16)
    lo = (r1 - mid.astype(F32)).astype(BF16)
    pieces = jnp.concatenate([hi, mid, lo], axis=0)
    t = lax.dot_general(pieces, place, (((0,), (0,)), ((), ())), preferred_element_type=F32)
    lane = lax.broadcasted_iota(jnp.int32, (1, HEAD_TILE), 1)
    is_rope = (lane % (HEAD_TILE // 2)) >= ROPE_LANE0
    cos_t = t[:, :HEAD_TILE] + jnp.where(is_rope, 0.0, 1.0)
    return cos_t, t[:, HEAD_TILE:]


def _rope_tile(x, cos_t, sin_t):
    return x * cos_t + pltpu.roll(x, HEAD_TILE // 2, 1) * sin_t


def _mla_proj_kernel(x_ref, pos_ref, inv_ref, place_ref, g_ref, wd_ref, qg_ref, wuq_ref, kvg_ref,
                     wuk_ref, wuv_ref, q_ref, k_ref, v_ref):
    x = x_ref[0]
    h = _rms(x, g_ref[...]).astype(BF16)
    c = _dot(h, wd_ref[...])
    cq = _rms(c[:, :Q_LORA], qg_ref[...]).astype(BF16)
    ckv = _rms(c[:, Q_LORA:Q_LORA + KV_LORA], kvg_ref[...]).astype(BF16)
    kpe = c[:, Q_LORA + KV_LORA:]

    cos_t, sin_t = _rope_tables(pos_ref[0], inv_ref[...], place_ref[...])
    kpe = _rope_tile(kpe, cos_t, sin_t)
    qscale = MLA_SCALE * LOG2E
    cos_q, sin_q = cos_t * qscale, sin_t * qscale
    q = _dot(cq, wuq_ref[...])
    kn = _dot(ckv, wuk_ref[...])
    for hd in range(MLA_HEADS):
        sl = slice(hd * HEAD_TILE, (hd + 1) * HEAD_TILE)
        q_ref[0, :, sl] = _rope_tile(q[:, sl], cos_q, sin_q).astype(BF16)
        k_ref[0, :, sl] = (kn[:, sl] + kpe).astype(BF16)
    vt = _dot_nt(wuv_ref[...], ckv)
    ones_tile = jnp.where(lax.broadcasted_iota(jnp.int32, (VT_ROWS - V_HEAD, BK), 0) == 0, 1.0, 0.0)
    for hd in range(MLA_HEADS):
        for kb in range(TB_PROJ // BK):
            v_ref[0, hd, kb, 0:V_HEAD, :] = vt[hd * V_HEAD:(hd + 1) * V_HEAD,
                                               kb * BK:(kb + 1) * BK].astype(BF16)
            v_ref[0, hd, kb, V_HEAD:VT_ROWS, :] = ones_tile.astype(BF16)


def _mla_proj(x, pos_rows, inv_col, place, g, wd, qg, wuq, kvg, wuk, wuv_t):
    b, t, _ = x.shape
    tb = TB_PROJ
    hw = MLA_HEADS * HEAD_TILE
    dcols = Q_LORA + KV_LORA + HEAD_TILE
    tok = pl.BlockSpec((1, tb, D_MODEL), lambda i, j: (i, j, 0))
    wide = pl.BlockSpec((1, tb, hw), lambda i, j: (i, j, 0))
    return pl.pallas_call(
        _mla_proj_kernel,
        out_shape=(
            jax.ShapeDtypeStruct((b, t, hw), BF16),
            jax.ShapeDtypeStruct((b, t, hw), BF16),
            jax.ShapeDtypeStruct((b, MLA_HEADS, t // BK, VT_ROWS, BK), BF16),
        ),
        grid=(b, t // tb),
        in_specs=[
            tok,
            pl.BlockSpec((1, 1, tb), lambda i, j: (i, 0, j)),
            _const_spec(inv_col.shape),
            _const_spec(place.shape),
            _const_spec((1, D_MODEL)),
            _const_spec((D_MODEL, dcols)),
            _const_spec((1, Q_LORA)),
            _const_spec((Q_LORA, hw)),
            _const_spec((1, KV_LORA)),
            _const_spec((KV_LORA, hw)),
            _const_spec((MLA_HEADS * V_HEAD, KV_LORA)),
        ],
        out_specs=(wide, wide,
                   pl.BlockSpec((1, MLA_HEADS, tb // BK, VT_ROWS, BK), lambda i, j: (i, 0, j, 0, 0))),
        compiler_params=_cparams(("arbitrary", "arbitrary")),
        name="mla_proj",
    )(x, pos_rows, inv_col, place, g.reshape(1, D_MODEL), wd, qg.reshape(1, Q_LORA), wuq,
      kvg.reshape(1, KV_LORA), wuk, wuv_t)


ATTN_HEADS_PER_STEP = 4


def _flash_kernel(q_ref, qn_ref, k_ref, vt_ref, o_ref, s_even, s_odd, smax0):
    qi = pl.program_id(2)
    nh = ATTN_HEADS_PER_STEP
    units = [(hd, c) for hd in range(nh) for c in range(2)]

    def scores_to(buf, j, u, queries=q_ref):
        hd, c = units[u]
        lanes = slice(hd * HEAD_TILE, (hd + 1) * HEAD_TILE)
        k0 = pl.multiple_of(j * BK, BK)
        s = _dot_nt(k_ref[0, pl.ds(k0, BK), lanes], queries[0, c * BK:(c + 1) * BK, lanes])
        buf[u] = s
        return jnp.max(s, axis=0, keepdims=True)

    @pl.when(qi == 0)
    def _():
        for u in range(len(units)):
            smax0[u] = scores_to(s_even, 0, u)

    def online_update(s, vt, m, acc, smax):
        m_new = jnp.maximum(m, smax)
        alpha = jnp.exp2(m - m_new)
        p = jnp.exp2(s - m_new).astype(BF16)
        return m_new, alpha * acc + _dot(vt, p)

    def softmax_pv(buf, j, u, m, acc, smax, diagonal=False):
        vt = vt_ref[0, units[u][0], j]
        if not diagonal:
            return online_update(buf[u], vt, m, acc, smax)
        hk = BK // 2
        new = []
        for keys, q0 in ((hk, 0), (BK, hk)):
            s = buf[u, 0:keys, q0:q0 + hk]
            kv = lax.broadcasted_iota(jnp.int32, (keys, hk), 0)
            qq = lax.broadcasted_iota(jnp.int32, (keys, hk), 1) + q0
            s = jnp.where(kv <= qq, s, NEG)
            new.append(online_update(s, vt[:, 0:keys], m[:, q0:q0 + hk], acc[:, q0:q0 + hk],
                                     jnp.max(s, axis=0, keepdims=True)))
        return (jnp.concatenate([new[0][0], new[1][0]], axis=1),
                jnp.concatenate([new[0][1], new[1][1]], axis=1))

    def half_step(cur, nxt, j, carry):
        ms, accs, smax = (list(t) for t in carry)
        smax_next = []
        for u in range(len(units)):
            smax_next.append(scores_to(nxt, j + 1, u))
            ms[u], accs[u] = softmax_pv(cur, j, u, ms[u], accs[u], smax[u])
        return tuple(ms), tuple(accs), tuple(smax_next)

    def body(jj, carry):
        carry = half_step(s_even, s_odd, 2 * jj, carry)
        return half_step(s_odd, s_even, 2 * jj + 1, carry)

    init = (tuple(jnp.full((1, BK), NEG, F32) for _ in units),
            tuple(jnp.zeros((VT_ROWS, BK), F32) for _ in units),
            tuple(smax0[u] for u in range(len(units))))
    ms, accs, smax = lax.fori_loop(0, qi, body, init)

    ms, accs = list(ms), list(accs)
    for u, (hd, c) in enumerate(units):
        if c == 1:
            scores_to(s_odd, 2 * qi + 1, u)
    for u, (hd, c) in enumerate(units):
        ms[u], accs[u] = softmax_pv(s_even, 2 * qi, u, ms[u], accs[u], smax[u], diagonal=(c == 0))
        smax0[u] = scores_to(s_even, 0, u, qn_ref)
    for u, (hd, c) in enumerate(units):
        if c == 1:
            _, accs[u] = softmax_pv(s_odd, 2 * qi + 1, u, ms[u], accs[u], None, diagonal=True)
    for c in range(2):
        o_t = jnp.concatenate([accs[u][:V_HEAD] / accs[u][V_HEAD:V_HEAD + 1]
                               for u, (hd, cc) in enumerate(units) if cc == c], axis=0)
        o_ref[0, c * BK:(c + 1) * BK, :] = o_t.T.astype(BF16)


def _mla_attn(q, k, vt):
    b, t, _ = q.shape
    assert BQ == 2 * BK
    nh = ATTN_HEADS_PER_STEP
    nq = t // BQ
    return pl.pallas_call(
        _flash_kernel,
        out_shape=jax.ShapeDtypeStruct((b, t, MLA_HEADS * V_HEAD), BF16),
        grid=(b, MLA_HEADS // nh, nq),
        in_specs=[
            pl.BlockSpec((1, BQ, nh * HEAD_TILE), lambda i, hp, j: (i, j, hp)),
            pl.BlockSpec((1, BQ, nh * HEAD_TILE), lambda i, hp, j: (i, jnp.minimum(j + 1, nq - 1), hp)),
            pl.BlockSpec((1, t, nh * HEAD_TILE), lambda i, hp, j: (i, 0, hp)),
            pl.BlockSpec((1, nh, t // BK, VT_ROWS, BK), lambda i, hp, j: (i, hp, 0, 0, 0)),
        ],
        out_specs=pl.BlockSpec((1, BQ, nh * V_HEAD), lambda i, hp, j: (i, j, hp)),
        scratch_shapes=[pltpu.VMEM((2 * nh, BK, BK), F32), pltpu.VMEM((2 * nh, BK, BK), F32),
                        pltpu.VMEM((2 * nh, 1, BK), F32)],
        compiler_params=_cparams(("arbitrary", "arbitrary", "arbitrary")),
        name="mla_attn",
    )(q, q, k, vt)


def _head_tile_lanes():
    half = QK_ROPE // 2
    nope = np.concatenate([np.arange(ROPE_LANE0), HEAD_TILE // 2 + np.arange(QK_NOPE - ROPE_LANE0)])
    x1 = ROPE_LANE0 + np.arange(half)
    return np.concatenate([nope, x1, x1 + HEAD_TILE // 2])


def _to_head_tile(w):
    tile = jnp.zeros(w.shape[:-1] + (HEAD_TILE,), F32)
    return tile.at[..., _head_tile_lanes()].set(w)


def _mla_weights(w_dq_dkv, w_uq, w_ukv):
    kpe_w = w_dq_dkv[:, Q_LORA + KV_LORA:]
    kpe_tile = _to_head_tile(jnp.concatenate([jnp.zeros((D_MODEL, QK_NOPE), F32), kpe_w], axis=1))
    wd = jnp.concatenate([w_dq_dkv[:, :Q_LORA + KV_LORA], kpe_tile], axis=1)
    wuq = _to_head_tile(w_uq.reshape(Q_LORA, MLA_HEADS, QK_NOPE + QK_ROPE))
    wuq = wuq.reshape(Q_LORA, MLA_HEADS * HEAD_TILE)
    wkv = w_ukv.reshape(KV_LORA, MLA_HEADS, QK_NOPE + V_HEAD)
    wuk = _to_head_tile(jnp.pad(wkv[:, :, :QK_NOPE], ((0, 0), (0, 0), (0, QK_ROPE))))
    wuk = wuk.reshape(KV_LORA, MLA_HEADS * HEAD_TILE)
    wuv_t = wkv[:, :, QK_NOPE:].reshape(KV_LORA, MLA_HEADS * V_HEAD).T
    return tuple(w.astype(BF16) for w in (wd, wuq, wuk, wuv_t))


def _rope_constants():
    half = QK_ROPE // 2
    inv = 1.0 / (ROPE_THETA ** (np.arange(0, QK_ROPE, 2, dtype=np.float32) / QK_ROPE))
    lanes = _head_tile_lanes()
    x1, x2 = lanes[QK_NOPE:QK_NOPE + half], lanes[QK_NOPE + half:]
    place = np.zeros((2 * half, 2 * HEAD_TILE), np.float32)
    for f in range(half):
        place[f, x1[f]] = 1.0
        place[f, x2[f]] = 1.0
        place[half + f, HEAD_TILE + x1[f]] = -1.0
        place[half + f, HEAD_TILE + x2[f]] = 1.0
    return (jnp.asarray(inv.reshape(half, 1)), jnp.asarray(np.tile(place, (3, 1)), dtype=BF16))


def kernel(x, mem, positions, norm_mix_g, norm_xa_g, norm_mem_g, xa_wq, xa_wkv, xa_wo, norm_ffn_g, ffn_w_up, ffn_conv_w, ffn_conv_b, ffn_w_down, pc_w_in, pool_w, pool_scale, conv_dw_w, conv_dw_b, conv_ln_g, conv_ln_b, pc_w_out, mla_w_dq_dkv, mla_q_norm_g, mla_w_uq, mla_kv_norm_g, mla_w_ukv, mla_w_o, final_norm_g):
    depth = xa_wq.shape[0]
    b, t, _ = x.shape
    bf = lambda a: a.astype(BF16)
    mem_k, mem_v = _mem_kv(mem, norm_mem_g, bf(xa_wkv))
    xa_wq_b, xa_wo_b = bf(xa_wq), bf(xa_wo)
    w_up_b, w_down_b = bf(ffn_w_up), bf(ffn_w_down)
    pc_w_in_b, pool_w_b, pc_w_out_b = bf(pc_w_in), bf(pool_w), bf(pc_w_out)
    mla_w_o_b = bf(mla_w_o)
    pos_rows = positions.reshape(b, 1, t)
    inv_col, place = _rope_constants()

    for l in range(depth):
        if l % 2 == 0:
            e = l // 2
            x = _pc_mixer(x, norm_mix_g[l], pc_w_in_b[e], pool_w_b[e], pool_scale[e], conv_dw_w[e],
                          conv_dw_b[e], conv_ln_g[e], conv_ln_b[e], pc_w_out_b[e])
            x = _xattn(x, norm_xa_g[l], xa_wq_b[l], mem_k[l], mem_v[l], xa_wo_b[l])
        else:
            o = l // 2
            wd, wuq, wuk, wuv_t = _mla_weights(mla_w_dq_dkv[o], mla_w_uq[o], mla_w_ukv[o])
            q, k, vt = _mla_proj(x, pos_rows, inv_col, place, norm_mix_g[l], wd, mla_q_norm_g[o],
                                 wuq, mla_kv_norm_g[o], wuk, wuv_t)
            attn = _mla_attn(q, k, vt)
            x = _xattn(x, norm_xa_g[l], xa_wq_b[l], mem_k[l], mem_v[l], xa_wo_b[l],
                       attn_o=attn, wo_mla=mla_w_o_b[o])
        x = _ffn(x, norm_ffn_g[l], w_up_b[l], ffn_conv_w[l], ffn_conv_b[l], w_down_b[l],
                 final_norm_g, final_norm=(l == depth - 1))
    return x
```

```python
import functools
import math

import jax
import jax.numpy as jnp
import numpy as np
from jax import lax
from jax.experimental import pallas as pl
from jax.experimental.pallas import tpu as pltpu

D_MODEL = 1024
MEM_LEN = 256
XA_HEADS = 4
XA_HEAD_DIM = D_MODEL // XA_HEADS
POOL_W = 512
POOL_WINDOWS = (2, 4, 8, 16)
POOL_GROUP = 128
CONV_W = 512
CONV_K = 31
MLA_HEADS = 16
QK_NOPE = 64
QK_ROPE = 32
V_HEAD = 64
Q_LORA = 384
KV_LORA = 256
ROPE_THETA = 10000.0
MLA_SCALE = 1.0 / math.sqrt(QK_NOPE + QK_ROPE)
D_FF = 2816
FFN_CONV_K = 3
EPS = 1e-6
NEG = -1e30
LOG2E = 1.4426950408889634

LANES = 128
SUBLANES = 8
VMEM_LIMIT = 56 * 1024 * 1024

TB = 512
TB_FFN = 512
TB_XA = 1024
TB_PROJ = 1024
FF_CHUNK = 256
POOL_HALO = 16
CONV_HALO = 32
CONV_ROWS = 64
BK = 512
BQ = 2 * BK
HEAD_TILE = LANES
ROPE_LANE0 = 48
VT_ROWS = V_HEAD + 16

BF16 = jnp.bfloat16
F32 = jnp.float32


def _cparams(sem):
    return pltpu.CompilerParams(dimension_semantics=sem, vmem_limit_bytes=VMEM_LIMIT)


def _rms(x, g):
    return x * lax.rsqrt(jnp.mean(x * x, axis=-1, keepdims=True) + EPS) * g


def _dot(a, b):
    return jnp.dot(a, b, preferred_element_type=F32)


def _dot_nt(a, b):
    return lax.dot_general(a, b, (((1,), (1,)), ((), ())), preferred_element_type=F32)


def _const_spec(shape):
    n = len(shape)
    return pl.BlockSpec(shape, lambda *_: (0,) * n, pipeline_mode=pl.Buffered(1))


def _layer_spec(stacked, layer):
    shape = stacked.shape[1:]
    n = len(shape)
    return pl.BlockSpec((None,) + shape, lambda *_: (layer,) + (0,) * n,
                        pipeline_mode=pl.Buffered(1))


def _rows(stacked):
    return stacked.reshape(stacked.shape[0], 1, stacked.shape[1])


def _mem_kv_kernel(mem_ref, g_ref, wkv_ref, k_ref, v_ref):
    m = _rms(mem_ref[0], g_ref[0]).astype(BF16)
    kv = _dot(m, wkv_ref[0])
    k_ref[0, 0] = kv[:, :D_MODEL].astype(BF16)
    v_ref[0, 0] = kv[:, D_MODEL:].astype(BF16)


def _mem_kv(mem, norm_mem_g, xa_wkv):
    depth = xa_wkv.shape[0]
    b = mem.shape[0]
    out = jax.ShapeDtypeStruct((depth, b, MEM_LEN, D_MODEL), BF16)
    return pl.pallas_call(
        _mem_kv_kernel,
        out_shape=(out, out),
        grid=(depth, b),
        in_specs=[
            pl.BlockSpec((1, MEM_LEN, D_MODEL), lambda l, i: (i, 0, 0)),
            pl.BlockSpec((1, 1, D_MODEL), lambda l, i: (l, 0, 0)),
            pl.BlockSpec((1, D_MODEL, 2 * D_MODEL), lambda l, i: (l, 0, 0)),
        ],
        out_specs=(
            pl.BlockSpec((1, 1, MEM_LEN, D_MODEL), lambda l, i: (l, i, 0, 0)),
            pl.BlockSpec((1, 1, MEM_LEN, D_MODEL), lambda l, i: (l, i, 0, 0)),
        ),
        compiler_params=_cparams(("arbitrary", "arbitrary")),
        name="mem_kv",
    )(mem, norm_mem_g.reshape(depth, 1, D_MODEL), xa_wkv)


def _xattn_kernel(*refs, with_attn_proj):
    if with_attn_proj:
        x_ref, o_ref, wo_mla_ref, g_ref, wq_ref, k_ref, v_ref, wo_ref, out_ref = refs
        x = x_ref[0] + _dot(o_ref[0], wo_mla_ref[...])
    else:
        x_ref, g_ref, wq_ref, k_ref, v_ref, wo_ref, out_ref = refs
        x = x_ref[0]
    h = _rms(x, g_ref[...]).astype(BF16)
    q = (_dot(h, wq_ref[...]) * (XA_HEAD_DIM ** -0.5)).astype(BF16)
    heads = []
    for hd in range(XA_HEADS):
        sl = slice(hd * XA_HEAD_DIM, (hd + 1) * XA_HEAD_DIM)
        s = _dot_nt(q[:, sl], k_ref[0, :, sl])
        m = jnp.max(s, axis=-1, keepdims=True)
        p = jnp.exp(s - m)
        l = jnp.sum(p, axis=-1, keepdims=True)
        o = _dot(p.astype(BF16), v_ref[0, :, sl]) / l
        heads.append(o.astype(BF16))
    o = jnp.concatenate(heads, axis=-1)
    out_ref[0] = x + _dot(o, wo_ref[...])


def _xattn(x, layer, g, wq, k, v, wo, attn_o=None, wo_mla=None, mla_layer=None):
    b, t, _ = x.shape
    with_attn_proj = attn_o is not None
    tok = pl.BlockSpec((1, TB_XA, D_MODEL), lambda i, j: (i, j, 0))
    mem = pl.BlockSpec((None, 1, MEM_LEN, D_MODEL), lambda i, j: (layer, i, 0, 0))
    in_specs = [tok]
    args = [x]
    if with_attn_proj:
        in_specs += [tok, _layer_spec(wo_mla, mla_layer)]
        args += [attn_o, wo_mla]
    in_specs += [_layer_spec(g, layer), _layer_spec(wq, layer), mem, mem, _layer_spec(wo, layer)]
    args += [g, wq, k, v, wo]
    return pl.pallas_call(
        functools.partial(_xattn_kernel, with_attn_proj=with_attn_proj),
        out_shape=jax.ShapeDtypeStruct(x.shape, F32),
        grid=(b, t // TB_XA),
        in_specs=in_specs,
        out_specs=tok,
        compiler_params=_cparams(("arbitrary", "arbitrary")),
        name="xattn",
    )(*args)


def _ffn_kernel(x_ref, g_ref, wup_ref, cw_ref, cb_ref, wdn_ref, fg_ref, out_ref,
                tail, act, *, final_norm, tb):
    n_chunks = D_FF // FF_CHUNK

    @pl.when(pl.program_id(1) == 0)
    def _():
        tail[...] = jnp.zeros_like(tail)

    def delayed(gt, prev, d):
        row = lax.broadcasted_iota(jnp.int32, (SUBLANES, FF_CHUNK), 0)
        rolled = pltpu.roll(gt, d, 0)
        head = jnp.where(row < d, pltpu.roll(prev, d, 0), rolled[:SUBLANES])
        return jnp.concatenate([head, rolled[SUBLANES:]], axis=0)

    x = x_ref[0]
    h = _rms(x, g_ref[...]).astype(BF16)
    for c in range(n_chunks):
        cs = slice(c * FF_CHUNK, (c + 1) * FF_CHUNK)
        gs = slice(D_FF + c * FF_CHUNK, D_FF + (c + 1) * FF_CHUNK)
        a = _dot(h, wup_ref[:, cs])
        gt = _dot(h, wup_ref[:, gs])
        prev = tail[:, cs]
        tail[:, cs] = gt[tb - SUBLANES:]
        cw = cw_ref[:, cs]
        conv = (cw[0:1] * delayed(gt, prev, 2) + cw[1:2] * delayed(gt, prev, 1)
                + cw[2:3] * gt + cb_ref[:, cs])
        act[:, cs] = (conv * jax.nn.sigmoid(conv) * a).astype(BF16)
    y = x + _dot(act[...], wdn_ref[...])
    if final_norm:
        y = _rms(y, fg_ref[...])
    out_ref[0] = y


def _ffn(x, layer, g, w_up, conv_w, conv_b, w_down, final_g, final_norm):
    b, t, _ = x.shape
    tb = TB_FFN
    tok = pl.BlockSpec((1, tb, D_MODEL), lambda i, j: (i, j, 0))
    return pl.pallas_call(
        functools.partial(_ffn_kernel, final_norm=final_norm, tb=tb),
        out_shape=jax.ShapeDtypeStruct(x.shape, F32),
        grid=(b, t // tb),
        in_specs=[
            tok,
            _layer_spec(g, layer),
            _layer_spec(w_up, layer),
            _layer_spec(conv_w, layer),
            _layer_spec(conv_b, layer),
            _layer_spec(w_down, layer),
            _const_spec((1, D_MODEL)),
        ],
        out_specs=tok,
        scratch_shapes=[
            pltpu.VMEM((SUBLANES, D_FF), F32),
            pltpu.VMEM((tb, D_FF), BF16),
        ],
        compiler_params=_cparams(("arbitrary", "arbitrary")),
        name="conv_ffn",
    )(x, g, w_up, conv_w, conv_b, w_down, final_g.reshape(1, D_MODEL))


def _pc_kernel(x_ref, g_ref, win_ref, pw_ref, ps_ref, dw_ref, db_ref, lg_ref, lb_ref, wout_ref,
               out_ref, ubuf, glbuf, shbuf, ybuf):
    tblk = pl.program_id(1)

    @pl.when(tblk == 0)
    def _():
        ubuf[0:POOL_HALO, :] = jnp.zeros((POOL_HALO, POOL_W), F32)
        glbuf[0:CONV_HALO, :] = jnp.zeros((CONV_HALO, CONV_W), F32)

    x = x_ref[0]
    h = _rms(x, g_ref[...]).astype(BF16)
    z = _dot(h, win_ref[...])
    u = z[:, :POOL_W]
    gl = z[:, POOL_W:POOL_W + CONV_W] * jax.nn.sigmoid(z[:, POOL_W + CONV_W:])
    ubuf[POOL_HALO:POOL_HALO + TB, :] = u
    glbuf[CONV_HALO:CONV_HALO + TB, :] = gl

    pos = tblk * TB + lax.broadcasted_iota(jnp.int32, (TB, 1), 0)
    for gi, w in enumerate(POOL_WINDOWS):
        ls = slice(gi * POOL_GROUP, (gi + 1) * POOL_GROUP)
        ue = ubuf[:, ls]
        s8 = ue
        for d in range(1, min(w, SUBLANES)):
            s8 = s8 + pltpu.roll(ue, d, 0)
        ssum = s8[POOL_HALO:]
        if w > SUBLANES:
            assert w == 2 * SUBLANES
            ssum = ssum + s8[POOL_HALO - SUBLANES:POOL_HALO - SUBLANES + TB]
        cnt = jnp.minimum(pos + 1, w).astype(F32)
        pooled = (ssum / cnt - u[:, ls]).astype(BF16)
        ya = _dot(pooled, pw_ref[gi]) * ps_ref[:, ls]
        ybuf[:, ls] = ya.astype(BF16)

    lo = SUBLANES
    gle = glbuf[...]
    for r in range(1, SUBLANES):
        shbuf[r - 1, lo:CONV_HALO + TB, :] = pltpu.roll(gle, r, 0)[lo:]
    tiles = CONV_ROWS // SUBLANES
    for c0 in range(0, TB, CONV_ROWS):
        cv = jnp.zeros((tiles, SUBLANES, CONV_W), F32) + db_ref[...]
        for j in range(CONV_K):
            a, r = divmod(CONV_K - 1 - j, SUBLANES)
            start = CONV_HALO - SUBLANES * a + c0
            src = glbuf if r == 0 else shbuf.at[r - 1]
            win = src[start:start + CONV_ROWS, :].reshape(tiles, SUBLANES, CONV_W)
            cv = cv + dw_ref[j] * win
        mu = jnp.mean(cv, axis=-1, keepdims=True)
        xc = cv - mu
        yn = (xc * lax.rsqrt(jnp.mean(xc * xc, axis=-1, keepdims=True) + EPS) * lg_ref[...]
              + lb_ref[...])
        act = (yn * jax.nn.sigmoid(yn)).reshape(CONV_ROWS, CONV_W)
        ybuf[c0:c0 + CONV_ROWS, POOL_W:] = act.astype(BF16)

    ubuf[0:POOL_HALO, :] = ubuf[TB:TB + POOL_HALO, :]
    glbuf[0:CONV_HALO, :] = glbuf[TB:TB + CONV_HALO, :]

    out_ref[0] = x + _dot(ybuf[...], wout_ref[...])


def _pc_mixer(x, layer, g, e, w_in, pool_w, pool_scale, dw_w, dw_b, ln_g, ln_b, w_out):
    b, t, _ = x.shape
    tok = pl.BlockSpec((1, TB, D_MODEL), lambda i, j: (i, j, 0))
    params = (w_in, pool_w, pool_scale, dw_w, dw_b, ln_g, ln_b, w_out)
    return pl.pallas_call(
        _pc_kernel,
        out_shape=jax.ShapeDtypeStruct(x.shape, F32),
        grid=(b, t // TB),
        in_specs=[tok, _layer_spec(g, layer)] + [_layer_spec(p, e) for p in params],
        out_specs=tok,
        scratch_shapes=[
            pltpu.VMEM((POOL_HALO + TB, POOL_W), F32),
            pltpu.VMEM((CONV_HALO + TB, CONV_W), F32),
            pltpu.VMEM((SUBLANES - 1, CONV_HALO + TB, CONV_W), F32),
            pltpu.VMEM((TB, POOL_W + CONV_W), BF16),
        ],
        compiler_params=_cparams(("arbitrary", "arbitrary")),
        name="pc_mixer",
    )(x, g, *params)


def _rope_tables(pos_row, inv_col, place):
    ang = inv_col * pos_row.astype(F32)
    cs = jnp.concatenate([jnp.cos(ang), jnp.sin(ang)], axis=0)
    hi = cs.astype(BF16)
    r1 = cs - hi.astype(F32)
    mid = r1.astype(BF16)
    lo = (r1 - mid.astype(F32)).astype(BF16)
    pieces = jnp.concatenate([hi, mid, lo], axis=0)
    t = lax.dot_general(pieces, place, (((0,), (0,)), ((), ())), preferred_element_type=F32)
    lane = lax.broadcasted_iota(jnp.int32, (1, HEAD_TILE), 1)
    is_rope = (lane % (HEAD_TILE // 2)) >= ROPE_LANE0
    cos_t = t[:, :HEAD_TILE] + jnp.where(is_rope, 0.0, 1.0)
    return cos_t, t[:, HEAD_TILE:]


def _rope_tile(x, cos_t, sin_t):
    return x * cos_t + pltpu.roll(x, HEAD_TILE // 2, 1) * sin_t


def _mla_proj_kernel(x_ref, pos_ref, inv_ref, place_ref, g_ref, wd_ref, qg_ref, wuq_ref, kvg_ref,
                     wuk_ref, wuv_ref, q_ref, k_ref, v_ref):
    x = x_ref[0]
    h = _rms(x, g_ref[...]).astype(BF16)
    c = _dot(h, wd_ref[...])
    cq = _rms(c[:, :Q_LORA], qg_ref[...]).astype(BF16)
    ckv = _rms(c[:, Q_LORA:Q_LORA + KV_LORA], kvg_ref[...]).astype(BF16)
    kpe = c[:, Q_LORA + KV_LORA:]

    cos_t, sin_t = _rope_tables(pos_ref[0], inv_ref[...], place_ref[...])
    kpe = _rope_tile(kpe, cos_t, sin_t)
    qscale = MLA_SCALE * LOG2E
    cos_q, sin_q = cos_t * qscale, sin_t * qscale
    q = _dot(cq, wuq_ref[...])
    kn = _dot(ckv, wuk_ref[...])
    for hd in range(MLA_HEADS):
        sl = slice(hd * HEAD_TILE, (hd + 1) * HEAD_TILE)
        q_ref[0, :, sl] = _rope_tile(q[:, sl], cos_q, sin_q).astype(BF16)
        k_ref[0, :, sl] = (kn[:, sl] + kpe).astype(BF16)
    vt = _dot_nt(wuv_ref[...], ckv)
    ones_tile = jnp.where(lax.broadcasted_iota(jnp.int32, (VT_ROWS - V_HEAD, BK), 0) == 0, 1.0, 0.0)
    for hd in range(MLA_HEADS):
        for kb in range(TB_PROJ // BK):
            v_ref[0, hd, kb, 0:V_HEAD, :] = vt[hd * V_HEAD:(hd + 1) * V_HEAD,
                                               kb * BK:(kb + 1) * BK].astype(BF16)
            v_ref[0, hd, kb, V_HEAD:VT_ROWS, :] = ones_tile.astype(BF16)


def _mla_proj(x, pos_rows, inv_col, place, layer, g, o, wd, qg, wuq, kvg, wuk, wuv_t):
    b, t, _ = x.shape
    tb = TB_PROJ
    hw = MLA_HEADS * HEAD_TILE
    tok = pl.BlockSpec((1, tb, D_MODEL), lambda i, j: (i, j, 0))
    wide = pl.BlockSpec((1, tb, hw), lambda i, j: (i, j, 0))
    return pl.pallas_call(
        _mla_proj_kernel,
        out_shape=(
            jax.ShapeDtypeStruct((b, t, hw), BF16),
            jax.ShapeDtypeStruct((b, t, hw), BF16),
            jax.ShapeDtypeStruct((b, MLA_HEADS, t // BK, VT_ROWS, BK), BF16),
        ),
        grid=(b, t // tb),
        in_specs=[
            tok,
            pl.BlockSpec((1, 1, tb), lambda i, j: (i, 0, j)),
            _const_spec(inv_col.shape),
            _const_spec(place.shape),
            _layer_spec(g, layer),
        ] + [_layer_spec(p, o) for p in (wd, qg, wuq, kvg, wuk, wuv_t)],
        out_specs=(wide, wide,
                   pl.BlockSpec((1, MLA_HEADS, tb // BK, VT_ROWS, BK), lambda i, j: (i, 0, j, 0, 0))),
        compiler_params=_cparams(("arbitrary", "arbitrary")),
        name="mla_proj",
    )(x, pos_rows, inv_col, place, g, wd, qg, wuq, kvg, wuk, wuv_t)


ATTN_HEADS_PER_STEP = 4


def _flash_kernel(q_ref, qn_ref, k_ref, vt_ref, o_ref, s_even, s_odd, smax0):
    qi = pl.program_id(2)
    nh = ATTN_HEADS_PER_STEP
    units = [(hd, c) for hd in range(nh) for c in range(2)]

    def scores_to(buf, j, u, queries=q_ref):
        hd, c = units[u]
        lanes = slice(hd * HEAD_TILE, (hd + 1) * HEAD_TILE)
        k0 = pl.multiple_of(j * BK, BK)
        s = _dot_nt(k_ref[0, pl.ds(k0, BK), lanes], queries[0, c * BK:(c + 1) * BK, lanes])
        buf[u] = s
        return jnp.max(s, axis=0, keepdims=True)

    @pl.when(qi == 0)
    def _():
        for u in range(len(units)):
            smax0[u] = scores_to(s_even, 0, u)

    def online_update(s, vt, m, acc, smax):
        m_new = jnp.maximum(m, smax)
        alpha = jnp.exp2(m - m_new)
        p = jnp.exp2(s - m_new).astype(BF16)
        return m_new, alpha * acc + _dot(vt, p)

    def softmax_pv(buf, j, u, m, acc, smax, diagonal=False):
        vt = vt_ref[0, units[u][0], j]
        if not diagonal:
            return online_update(buf[u], vt, m, acc, smax)
        hk = BK // 2
        new = []
        for keys, q0 in ((hk, 0), (BK, hk)):
            s = buf[u, 0:keys, q0:q0 + hk]
            kv = lax.broadcasted_iota(jnp.int32, (keys, hk), 0)
            qq = lax.broadcasted_iota(jnp.int32, (keys, hk), 1) + q0
            s = jnp.where(kv <= qq, s, NEG)
            new.append(online_update(s, vt[:, 0:keys], m[:, q0:q0 + hk], acc[:, q0:q0 + hk],
                                     jnp.max(s, axis=0, keepdims=True)))
        return (jnp.concatenate([new[0][0], new[1][0]], axis=1),
                jnp.concatenate([new[0][1], new[1][1]], axis=1))

    def half_step(cur, nxt, j, carry):
        ms, accs, smax = (list(t) for t in carry)
        smax_next = []
        for u in range(len(units)):
            smax_next.append(scores_to(nxt, j + 1, u))
            ms[u], accs[u] = softmax_pv(cur, j, u, ms[u], accs[u], smax[u])
        return tuple(ms), tuple(accs), tuple(smax_next)

    def body(jj, carry):
        carry = half_step(s_even, s_odd, 2 * jj, carry)
        return half_step(s_odd, s_even, 2 * jj + 1, carry)

    init = (tuple(jnp.full((1, BK), NEG, F32) for _ in units),
            tuple(jnp.zeros((VT_ROWS, BK), F32) for _ in units),
            tuple(smax0[u] for u in range(len(units))))
    ms, accs, smax = lax.fori_loop(0, qi, body, init)

    ms, accs = list(ms), list(accs)
    for u, (hd, c) in enumerate(units):
        if c == 1:
            scores_to(s_odd, 2 * qi + 1, u)
    for u, (hd, c) in enumerate(units):
        ms[u], accs[u] = softmax_pv(s_even, 2 * qi, u, ms[u], accs[u], smax[u], diagonal=(c == 0))
        smax0[u] = scores_to(s_even, 0, u, qn_ref)
    for u, (hd, c) in enumerate(units):
        if c == 1:
            _, accs[u] = softmax_pv(s_odd, 2 * qi + 1, u, ms[u], accs[u], None, diagonal=True)
    for c in range(2):
        o_t = jnp.concatenate([accs[u][:V_HEAD] / accs[u][V_HEAD:V_HEAD + 1]
                               for u, (hd, cc) in enumerate(units) if cc == c], axis=0)
        o_ref[0, c * BK:(c + 1) * BK, :] = o_t.T.astype(BF16)


def _mla_attn(q, k, vt):
    b, t, _ = q.shape
    assert BQ == 2 * BK
    nh = ATTN_HEADS_PER_STEP
    nq = t // BQ
    return pl.pallas_call(
        _flash_kernel,
        out_shape=jax.ShapeDtypeStruct((b, t, MLA_HEADS * V_HEAD), BF16),
        grid=(b, MLA_HEADS // nh, nq),
        in_specs=[
            pl.BlockSpec((1, BQ, nh * HEAD_TILE), lambda i, hp, j: (i, j, hp)),
            pl.BlockSpec((1, BQ, nh * HEAD_TILE), lambda i, hp, j: (i, jnp.minimum(j + 1, nq - 1), hp)),
            pl.BlockSpec((1, t, nh * HEAD_TILE), lambda i, hp, j: (i, 0, hp)),
            pl.BlockSpec((1, nh, t // BK, VT_ROWS, BK), lambda i, hp, j: (i, hp, 0, 0, 0)),
        ],
        out_specs=pl.BlockSpec((1, BQ, nh * V_HEAD), lambda i, hp, j: (i, j, hp)),
        scratch_shapes=[pltpu.VMEM((2 * nh, BK, BK), F32), pltpu.VMEM((2 * nh, BK, BK), F32),
                        pltpu.VMEM((2 * nh, 1, BK), F32)],
        compiler_params=_cparams(("arbitrary", "arbitrary", "arbitrary")),
        name="mla_attn",
    )(q, q, k, vt)


def _head_tile_lanes():
    half = QK_ROPE // 2
    nope = np.concatenate([np.arange(ROPE_LANE0), HEAD_TILE // 2 + np.arange(QK_NOPE - ROPE_LANE0)])
    x1 = ROPE_LANE0 + np.arange(half)
    return np.concatenate([nope, x1, x1 + HEAD_TILE // 2])


def _to_head_tile(w):
    tile = jnp.zeros(w.shape[:-1] + (HEAD_TILE,), F32)
    return tile.at[..., _head_tile_lanes()].set(w)


def _mla_weights(w_dq_dkv, w_uq, w_ukv):
    n = w_dq_dkv.shape[0]
    kpe_w = w_dq_dkv[..., Q_LORA + KV_LORA:]
    kpe_tile = _to_head_tile(jnp.concatenate([jnp.zeros((n, D_MODEL, QK_NOPE), F32), kpe_w], axis=-1))
    wd = jnp.concatenate([w_dq_dkv[..., :Q_LORA + KV_LORA], kpe_tile], axis=-1)
    wuq = _to_head_tile(w_uq.reshape(n, Q_LORA, MLA_HEADS, QK_NOPE + QK_ROPE))
    wuq = wuq.reshape(n, Q_LORA, MLA_HEADS * HEAD_TILE)
    wkv = w_ukv.reshape(n, KV_LORA, MLA_HEADS, QK_NOPE + V_HEAD)
    wuk = _to_head_tile(jnp.pad(wkv[..., :QK_NOPE], ((0, 0), (0, 0), (0, 0), (0, QK_ROPE))))
    wuk = wuk.reshape(n, KV_LORA, MLA_HEADS * HEAD_TILE)
    wuv_t = wkv[..., QK_NOPE:].reshape(n, KV_LORA, MLA_HEADS * V_HEAD).transpose(0, 2, 1)
    return tuple(w.astype(BF16) for w in (wd, wuq, wuk, wuv_t))


def _rope_constants():
    half = QK_ROPE // 2
    inv = 1.0 / (ROPE_THETA ** (np.arange(0, QK_ROPE, 2, dtype=np.float32) / QK_ROPE))
    lanes = _head_tile_lanes()
    x1, x2 = lanes[QK_NOPE:QK_NOPE + half], lanes[QK_NOPE + half:]
    place = np.zeros((2 * half, 2 * HEAD_TILE), np.float32)
    for f in range(half):
        place[f, x1[f]] = 1.0
        place[f, x2[f]] = 1.0
        place[half + f, HEAD_TILE + x1[f]] = -1.0
        place[half + f, HEAD_TILE + x2[f]] = 1.0
    return (jnp.asarray(inv.reshape(half, 1)), jnp.asarray(np.tile(place, (3, 1)), dtype=BF16))


def kernel(x, mem, positions, norm_mix_g, norm_xa_g, norm_mem_g, xa_wq, xa_wkv, xa_wo, norm_ffn_g, ffn_w_up, ffn_conv_w, ffn_conv_b, ffn_w_down, pc_w_in, pool_w, pool_scale, conv_dw_w, conv_dw_b, conv_ln_g, conv_ln_b, pc_w_out, mla_w_dq_dkv, mla_q_norm_g, mla_w_uq, mla_kv_norm_g, mla_w_ukv, mla_w_o, final_norm_g):
    depth = xa_wq.shape[0]
    b, t, _ = x.shape
    bf = lambda a: a.astype(BF16)
    mem_k, mem_v = _mem_kv(mem, norm_mem_g, bf(xa_wkv))
    xa = (_rows(norm_xa_g), bf(xa_wq), mem_k, mem_v, bf(xa_wo))
    ffn = (_rows(norm_ffn_g), bf(ffn_w_up), ffn_conv_w, _rows(ffn_conv_b), bf(ffn_w_down))
    mix_g = _rows(norm_mix_g)
    n_even = pc_w_in.shape[0]
    pc = (bf(pc_w_in), bf(pool_w), _rows(pool_scale),
          jnp.broadcast_to(conv_dw_w[:, :, None, :], (n_even, CONV_K, SUBLANES, CONV_W)),
          _rows(conv_dw_b), _rows(conv_ln_g), _rows(conv_ln_b), bf(pc_w_out))
    wd, wuq, wuk, wuv_t = _mla_weights(mla_w_dq_dkv, mla_w_uq, mla_w_ukv)
    mla = (wd, _rows(mla_q_norm_g), wuq, _rows(mla_kv_norm_g), wuk, wuv_t)
    mla_w_o_b = bf(mla_w_o)
    pos_rows = positions.reshape(b, 1, t)
    inv_col, place = _rope_constants()

    for l in range(depth):
        if l % 2 == 0:
            x = _pc_mixer(x, l, mix_g, l // 2, *pc)
            x = _xattn(x, l, *xa)
        else:
            q, k, vt = _mla_proj(x, pos_rows, inv_col, place, l, mix_g, l // 2, *mla)
            attn = _mla_attn(q, k, vt)
            x = _xattn(x, l, *xa, attn_o=attn, wo_mla=mla_w_o_b, mla_layer=l // 2)
        x = _ffn(x, l, *ffn, final_norm_g, final_norm=(l == depth - 1))
    return x
```

```python
import functools
import math

import jax
import jax.numpy as jnp
import numpy as np
from jax import lax
from jax.experimental import pallas as pl
from jax.experimental.pallas import tpu as pltpu

D_MODEL = 1024
MEM_LEN = 256
XA_HEADS = 4
XA_HEAD_DIM = D_MODEL // XA_HEADS
POOL_W = 512
POOL_WINDOWS = (2, 4, 8, 16)
POOL_GROUP = 128
CONV_W = 512
CONV_K = 31
MLA_HEADS = 16
QK_NOPE = 64
QK_ROPE = 32
V_HEAD = 64
Q_LORA = 384
KV_LORA = 256
ROPE_THETA = 10000.0
MLA_SCALE = 1.0 / math.sqrt(QK_NOPE + QK_ROPE)
D_FF = 2816
FFN_CONV_K = 3
EPS = 1e-6
NEG = -1e30
LOG2E = 1.4426950408889634

LANES = 128
SUBLANES = 8
VMEM_LIMIT = 56 * 1024 * 1024

TB = 512
TB_FFN = 512
TB_XA = 1024
TB_PROJ = 1024
FF_CHUNK = 256
POOL_HALO = 16
CONV_HALO = 32
CONV_ROWS = 64
BK = 512
BQ = 2 * BK
HEAD_TILE = LANES
ROPE_LANE0 = 48
VT_ROWS = V_HEAD + 16

BF16 = jnp.bfloat16
F32 = jnp.float32


def _cparams(sem):
    return pltpu.CompilerParams(dimension_semantics=sem, vmem_limit_bytes=VMEM_LIMIT)


def _rms(x, g):
    return x * lax.rsqrt(jnp.mean(x * x, axis=-1, keepdims=True) + EPS) * g


def _dot(a, b):
    return jnp.dot(a, b, preferred_element_type=F32)


def _dot_nt(a, b):
    return lax.dot_general(a, b, (((1,), (1,)), ((), ())), preferred_element_type=F32)


def _const_spec(shape):
    n = len(shape)
    return pl.BlockSpec(shape, lambda *_: (0,) * n, pipeline_mode=pl.Buffered(1))


def _layer_spec(stacked, layer):
    shape = stacked.shape[1:]
    n = len(shape)
    return pl.BlockSpec((None,) + shape, lambda *_: (layer,) + (0,) * n,
                        pipeline_mode=pl.Buffered(1))


def _rows(stacked):
    return stacked.reshape(stacked.shape[0], 1, stacked.shape[1])


def _mem_kv_kernel(mem_ref, g_ref, wkv_ref, k_ref, v_ref):
    m = _rms(mem_ref[0], g_ref[0]).astype(BF16)
    kv = _dot(m, wkv_ref[0])
    k_ref[0, 0] = kv[:, :D_MODEL].astype(BF16)
    v_ref[0, 0] = kv[:, D_MODEL:].astype(BF16)


def _mem_kv(mem, norm_mem_g, xa_wkv):
    depth = xa_wkv.shape[0]
    b = mem.shape[0]
    out = jax.ShapeDtypeStruct((depth, b, MEM_LEN, D_MODEL), BF16)
    return pl.pallas_call(
        _mem_kv_kernel,
        out_shape=(out, out),
        grid=(depth, b),
        in_specs=[
            pl.BlockSpec((1, MEM_LEN, D_MODEL), lambda l, i: (i, 0, 0)),
            pl.BlockSpec((1, 1, D_MODEL), lambda l, i: (l, 0, 0)),
            pl.BlockSpec((1, D_MODEL, 2 * D_MODEL), lambda l, i: (l, 0, 0)),
        ],
        out_specs=(
            pl.BlockSpec((1, 1, MEM_LEN, D_MODEL), lambda l, i: (l, i, 0, 0)),
            pl.BlockSpec((1, 1, MEM_LEN, D_MODEL), lambda l, i: (l, i, 0, 0)),
        ),
        compiler_params=_cparams(("arbitrary", "arbitrary")),
        name="mem_kv",
    )(mem, norm_mem_g.reshape(depth, 1, D_MODEL), xa_wkv)


def _xattn_kernel(*refs, with_attn_proj):
    if with_attn_proj:
        x_ref, o_ref, wo_mla_ref, g_ref, wq_ref, k_ref, v_ref, wo_ref, out_ref = refs
        x = x_ref[0] + _dot(o_ref[0], wo_mla_ref[...])
    else:
        x_ref, g_ref, wq_ref, k_ref, v_ref, wo_ref, out_ref = refs
        x = x_ref[0]
    h = _rms(x, g_ref[...]).astype(BF16)
    q = (_dot(h, wq_ref[...]) * (XA_HEAD_DIM ** -0.5)).astype(BF16)
    heads = []
    for hd in range(XA_HEADS):
        sl = slice(hd * XA_HEAD_DIM, (hd + 1) * XA_HEAD_DIM)
        s = _dot_nt(q[:, sl], k_ref[0, :, sl])
        m = jnp.max(s, axis=-1, keepdims=True)
        p = jnp.exp(s - m)
        l = jnp.sum(p, axis=-1, keepdims=True)
        o = _dot(p.astype(BF16), v_ref[0, :, sl]) / l
        heads.append(o.astype(BF16))
    o = jnp.concatenate(heads, axis=-1)
    out_ref[0] = x + _dot(o, wo_ref[...])


def _xattn(x, layer, g, wq, k, v, wo, attn_o=None, wo_mla=None, mla_layer=None):
    b, t, _ = x.shape
    with_attn_proj = attn_o is not None
    tok = pl.BlockSpec((1, TB_XA, D_MODEL), lambda i, j: (i, j, 0))
    mem = pl.BlockSpec((None, 1, MEM_LEN, D_MODEL), lambda i, j: (layer, i, 0, 0))
    in_specs = [tok]
    args = [x]
    if with_attn_proj:
        in_specs += [tok, _layer_spec(wo_mla, mla_layer)]
        args += [attn_o, wo_mla]
    in_specs += [_layer_spec(g, layer), _layer_spec(wq, layer), mem, mem, _layer_spec(wo, layer)]
    args += [g, wq, k, v, wo]
    return pl.pallas_call(
        functools.partial(_xattn_kernel, with_attn_proj=with_attn_proj),
        out_shape=jax.ShapeDtypeStruct(x.shape, F32),
        grid=(b, t // TB_XA),
        in_specs=in_specs,
        out_specs=tok,
        compiler_params=_cparams(("arbitrary", "arbitrary")),
        name="xattn",
    )(*args)


def _ffn_kernel(x_ref, g_ref, wup_ref, cw_ref, cb_ref, wdn_ref, fg_ref, out_ref,
                tail, act, *, final_norm, tb):
    n_chunks = D_FF // FF_CHUNK

    @pl.when(pl.program_id(1) == 0)
    def _():
        tail[...] = jnp.zeros_like(tail)

    def delayed(gt, prev, d):
        row = lax.broadcasted_iota(jnp.int32, (SUBLANES, FF_CHUNK), 0)
        rolled = pltpu.roll(gt, d, 0)
        head = jnp.where(row < d, pltpu.roll(prev, d, 0), rolled[:SUBLANES])
        return jnp.concatenate([head, rolled[SUBLANES:]], axis=0)

    x = x_ref[0]
    h = _rms(x, g_ref[...]).astype(BF16)
    for c in range(n_chunks):
        cs = slice(c * FF_CHUNK, (c + 1) * FF_CHUNK)
        gs = slice(D_FF + c * FF_CHUNK, D_FF + (c + 1) * FF_CHUNK)
        a = _dot(h, wup_ref[:, cs])
        gt = _dot(h, wup_ref[:, gs])
        prev = tail[:, cs]
        tail[:, cs] = gt[tb - SUBLANES:]
        cw = cw_ref[:, cs]
        conv = (cw[0:1] * delayed(gt, prev, 2) + cw[1:2] * delayed(gt, prev, 1)
                + cw[2:3] * gt + cb_ref[:, cs])
        act[:, cs] = (conv * jax.nn.sigmoid(conv) * a).astype(BF16)
    y = x + _dot(act[...], wdn_ref[...])
    if final_norm:
        y = _rms(y, fg_ref[...])
    out_ref[0] = y


def _ffn(x, layer, g, w_up, conv_w, conv_b, w_down, final_g, final_norm):
    b, t, _ = x.shape
    tb = TB_FFN
    tok = pl.BlockSpec((1, tb, D_MODEL), lambda i, j: (i, j, 0))
    return pl.pallas_call(
        functools.partial(_ffn_kernel, final_norm=final_norm, tb=tb),
        out_shape=jax.ShapeDtypeStruct(x.shape, F32),
        grid=(b, t // tb),
        in_specs=[
            tok,
            _layer_spec(g, layer),
            _layer_spec(w_up, layer),
            _layer_spec(conv_w, layer),
            _layer_spec(conv_b, layer),
            _layer_spec(w_down, layer),
            _const_spec((1, D_MODEL)),
        ],
        out_specs=tok,
        scratch_shapes=[
            pltpu.VMEM((SUBLANES, D_FF), F32),
            pltpu.VMEM((tb, D_FF), BF16),
        ],
        compiler_params=_cparams(("arbitrary", "arbitrary")),
        name="conv_ffn",
    )(x, g, w_up, conv_w, conv_b, w_down, final_g.reshape(1, D_MODEL))


def _pc_kernel(x_ref, g_ref, win_ref, pw_ref, ps_ref, dw_ref, db_ref, lg_ref, lb_ref, wout_ref,
               out_ref, ubuf, glbuf, shbuf, ybuf):
    tblk = pl.program_id(1)

    @pl.when(tblk == 0)
    def _():
        ubuf[0:POOL_HALO, :] = jnp.zeros((POOL_HALO, POOL_W), F32)
        glbuf[0:CONV_HALO, :] = jnp.zeros((CONV_HALO, CONV_W), F32)

    x = x_ref[0]
    h = _rms(x, g_ref[...]).astype(BF16)
    z = _dot(h, win_ref[...])
    u = z[:, :POOL_W]
    gl = z[:, POOL_W:POOL_W + CONV_W] * jax.nn.sigmoid(z[:, POOL_W + CONV_W:])
    ubuf[POOL_HALO:POOL_HALO + TB, :] = u
    glbuf[CONV_HALO:CONV_HALO + TB, :] = gl

    pos = tblk * TB + lax.broadcasted_iota(jnp.int32, (TB, 1), 0)
    for gi, w in enumerate(POOL_WINDOWS):
        ls = slice(gi * POOL_GROUP, (gi + 1) * POOL_GROUP)
        ue = ubuf[:, ls]
        s8 = ue
        for d in range(1, min(w, SUBLANES)):
            s8 = s8 + pltpu.roll(ue, d, 0)
        ssum = s8[POOL_HALO:]
        if w > SUBLANES:
            assert w == 2 * SUBLANES
            ssum = ssum + s8[POOL_HALO - SUBLANES:POOL_HALO - SUBLANES + TB]
        cnt = jnp.minimum(pos + 1, w).astype(F32)
        pooled = (ssum / cnt - u[:, ls]).astype(BF16)
        ya = _dot(pooled, pw_ref[gi]) * ps_ref[:, ls]
        ybuf[:, ls] = ya.astype(BF16)

    lo = SUBLANES
    gle = glbuf[...]
    for r in range(1, SUBLANES):
        shbuf[r - 1, lo:CONV_HALO + TB, :] = pltpu.roll(gle, r, 0)[lo:]
    tiles = CONV_ROWS // SUBLANES
    for c0 in range(0, TB, CONV_ROWS):
        cv = jnp.zeros((tiles, SUBLANES, CONV_W), F32) + db_ref[...]
        for j in range(CONV_K):
            a, r = divmod(CONV_K - 1 - j, SUBLANES)
            start = CONV_HALO - SUBLANES * a + c0
            src = glbuf if r == 0 else shbuf.at[r - 1]
            win = src[start:start + CONV_ROWS, :].reshape(tiles, SUBLANES, CONV_W)
            cv = cv + dw_ref[j] * win
        mu = jnp.mean(cv, axis=-1, keepdims=True)
        xc = cv - mu
        yn = (xc * lax.rsqrt(jnp.mean(xc * xc, axis=-1, keepdims=True) + EPS) * lg_ref[...]
              + lb_ref[...])
        act = (yn * jax.nn.sigmoid(yn)).reshape(CONV_ROWS, CONV_W)
        ybuf[c0:c0 + CONV_ROWS, POOL_W:] = act.astype(BF16)

    ubuf[0:POOL_HALO, :] = ubuf[TB:TB + POOL_HALO, :]
    glbuf[0:CONV_HALO, :] = glbuf[TB:TB + CONV_HALO, :]

    out_ref[0] = x + _dot(ybuf[...], wout_ref[...])


def _pc_mixer(x, layer, g, e, w_in, pool_w, pool_scale, dw_w, dw_b, ln_g, ln_b, w_out):
    b, t, _ = x.shape
    tok = pl.BlockSpec((1, TB, D_MODEL), lambda i, j: (i, j, 0))
    params = (w_in, pool_w, pool_scale, dw_w, dw_b, ln_g, ln_b, w_out)
    return pl.pallas_call(
        _pc_kernel,
        out_shape=jax.ShapeDtypeStruct(x.shape, F32),
        grid=(b, t // TB),
        in_specs=[tok, _layer_spec(g, layer)] + [_layer_spec(p, e) for p in params],
        out_specs=tok,
        scratch_shapes=[
            pltpu.VMEM((POOL_HALO + TB, POOL_W), F32),
            pltpu.VMEM((CONV_HALO + TB, CONV_W), F32),
            pltpu.VMEM((SUBLANES - 1, CONV_HALO + TB, CONV_W), F32),
            pltpu.VMEM((TB, POOL_W + CONV_W), BF16),
        ],
        compiler_params=_cparams(("arbitrary", "arbitrary")),
        name="pc_mixer",
    )(x, g, *params)


def _rope_tables(pos_row, inv_col, place):
    ang = inv_col * pos_row.astype(F32)
    cs = jnp.concatenate([jnp.cos(ang), jnp.sin(ang)], axis=0)
    hi = cs.astype(BF16)
    r1 = cs - hi.astype(F32)
    mid = r1.astype(BF16)
    lo = (r1 - mid.astype(F32)).astype(BF16)
    pieces = jnp.concatenate([hi, mid, lo], axis=0)
    t = lax.dot_general(pieces, place, (((0,), (0,)), ((), ())), preferred_element_type=F32)
    lane = lax.broadcasted_iota(jnp.int32, (1, HEAD_TILE), 1)
    is_rope = (lane % (HEAD_TILE // 2)) >= ROPE_LANE0
    cos_t = t[:, :HEAD_TILE] + jnp.where(is_rope, 0.0, 1.0)
    half = QK_ROPE // 2
    cos_f, sin_f = cs[:half], cs[half:]
    gap = HEAD_TILE // 2 - half
    fill = lambda v: jnp.full((gap, cs.shape[1]), v, F32)
    cos_tt = jnp.concatenate([fill(1.0), cos_f, fill(1.0), cos_f], axis=0)
    sin_tt = jnp.concatenate([fill(0.0), -sin_f, fill(0.0), sin_f], axis=0)
    return cos_t, t[:, HEAD_TILE:], cos_tt, sin_tt


def _rope_tile(x, cos_t, sin_t):
    return x * cos_t + pltpu.roll(x, HEAD_TILE // 2, 1) * sin_t


def _mla_proj_kernel(x_ref, pos_ref, inv_ref, place_ref, g_ref, wd_ref, qg_ref, wuq_ref, kvg_ref,
                     wuk_ref, wuv_ref, q_ref, k_ref, v_ref):
    x = x_ref[0]
    h = _rms(x, g_ref[...]).astype(BF16)
    c = _dot(h, wd_ref[...])
    cq = _rms(c[:, :Q_LORA], qg_ref[...]).astype(BF16)
    ckv = _rms(c[:, Q_LORA:Q_LORA + KV_LORA], kvg_ref[...]).astype(BF16)
    kpe = c[:, Q_LORA + KV_LORA:]

    cos_t, sin_t, cos_tt, sin_tt = _rope_tables(pos_ref[0], inv_ref[...], place_ref[...])
    kpe = _rope_tile(kpe, cos_t, sin_t)
    qscale = MLA_SCALE * LOG2E
    cos_q, sin_q = cos_tt * qscale, sin_tt * qscale
    qt = _dot_nt(wuq_ref[...], cq)
    kn = _dot(ckv, wuk_ref[...])
    hh = HEAD_TILE // 2
    for hd in range(MLA_HEADS):
        sl = slice(hd * HEAD_TILE, (hd + 1) * HEAD_TILE)
        qh = qt[sl]
        swapped = jnp.concatenate([qh[hh:], qh[:hh]], axis=0)
        q_ref[0, sl, :] = (qh * cos_q + swapped * sin_q).astype(BF16)
        k_ref[0, :, sl] = (kn[:, sl] + kpe).astype(BF16)
    vt = _dot_nt(wuv_ref[...], ckv)
    ones_tile = jnp.where(lax.broadcasted_iota(jnp.int32, (VT_ROWS - V_HEAD, BK), 0) == 0, 1.0, 0.0)
    for hd in range(MLA_HEADS):
        for kb in range(TB_PROJ // BK):
            v_ref[0, hd, kb, 0:V_HEAD, :] = vt[hd * V_HEAD:(hd + 1) * V_HEAD,
                                               kb * BK:(kb + 1) * BK].astype(BF16)
            v_ref[0, hd, kb, V_HEAD:VT_ROWS, :] = ones_tile.astype(BF16)


def _mla_proj(x, pos_rows, inv_col, place, layer, g, o, wd, qg, wuq, kvg, wuk, wuv_t):
    b, t, _ = x.shape
    tb = TB_PROJ
    hw = MLA_HEADS * HEAD_TILE
    tok = pl.BlockSpec((1, tb, D_MODEL), lambda i, j: (i, j, 0))
    wide = pl.BlockSpec((1, tb, hw), lambda i, j: (i, j, 0))
    return pl.pallas_call(
        _mla_proj_kernel,
        out_shape=(
            jax.ShapeDtypeStruct((b, hw, t), BF16),
            jax.ShapeDtypeStruct((b, t, hw), BF16),
            jax.ShapeDtypeStruct((b, MLA_HEADS, t // BK, VT_ROWS, BK), BF16),
        ),
        grid=(b, t // tb),
        in_specs=[
            tok,
            pl.BlockSpec((1, 1, tb), lambda i, j: (i, 0, j)),
            _const_spec(inv_col.shape),
            _const_spec(place.shape),
            _layer_spec(g, layer),
        ] + [_layer_spec(p, o) for p in (wd, qg, wuq, kvg, wuk, wuv_t)],
        out_specs=(pl.BlockSpec((1, hw, tb), lambda i, j: (i, 0, j)), wide,
                   pl.BlockSpec((1, MLA_HEADS, tb // BK, VT_ROWS, BK), lambda i, j: (i, 0, j, 0, 0))),
        compiler_params=_cparams(("arbitrary", "arbitrary")),
        name="mla_proj",
    )(x, pos_rows, inv_col, place, g, wd, qg, wuq, kvg, wuk, wuv_t)


ATTN_HEADS_PER_STEP = 4


def _flash_kernel(q_ref, qn_ref, k_ref, vt_ref, o_ref, s_even, s_odd, smax0):
    qi = pl.program_id(2)
    nh = ATTN_HEADS_PER_STEP
    units = [(hd, c) for hd in range(nh) for c in range(2)]

    def scores_to(buf, j, u, queries=q_ref):
        hd, c = units[u]
        lanes = slice(hd * HEAD_TILE, (hd + 1) * HEAD_TILE)
        k0 = pl.multiple_of(j * BK, BK)
        s = _dot(k_ref[0, pl.ds(k0, BK), lanes], queries[0, lanes, c * BK:(c + 1) * BK])
        buf[u] = s
        return jnp.max(s, axis=0, keepdims=True)

    @pl.when(qi == 0)
    def _():
        for u in range(len(units)):
            smax0[u] = scores_to(s_even, 0, u)

    def online_update(s, vt, m, acc, smax):
        m_new = jnp.maximum(m, smax)
        alpha = jnp.exp2(m - m_new)
        p = jnp.exp2(s - m_new).astype(BF16)
        return m_new, alpha * acc + _dot(vt, p)

    def softmax_pv(buf, j, u, m, acc, smax, diagonal=False):
        vt = vt_ref[0, units[u][0], j]
        if not diagonal:
            return online_update(buf[u], vt, m, acc, smax)
        hk = BK // 2
        new = []
        for keys, q0 in ((hk, 0), (BK, hk)):
            s = buf[u, 0:keys, q0:q0 + hk]
            kv = lax.broadcasted_iota(jnp.int32, (keys, hk), 0)
            qq = lax.broadcasted_iota(jnp.int32, (keys, hk), 1) + q0
            s = jnp.where(kv <= qq, s, NEG)
            new.append(online_update(s, vt[:, 0:keys], m[:, q0:q0 + hk], acc[:, q0:q0 + hk],
                                     jnp.max(s, axis=0, keepdims=True)))
        return (jnp.concatenate([new[0][0], new[1][0]], axis=1),
                jnp.concatenate([new[0][1], new[1][1]], axis=1))

    def half_step(cur, nxt, j, carry):
        ms, accs, smax = (list(t) for t in carry)
        smax_next = []
        for u in range(len(units)):
            smax_next.append(scores_to(nxt, j + 1, u))
            ms[u], accs[u] = softmax_pv(cur, j, u, ms[u], accs[u], smax[u])
        return tuple(ms), tuple(accs), tuple(smax_next)

    def body(jj, carry):
        carry = half_step(s_even, s_odd, 2 * jj, carry)
        return half_step(s_odd, s_even, 2 * jj + 1, carry)

    init = (tuple(jnp.full((1, BK), NEG, F32) for _ in units),
            tuple(jnp.zeros((VT_ROWS, BK), F32) for _ in units),
            tuple(smax0[u] for u in range(len(units))))
    ms, accs, smax = lax.fori_loop(0, qi, body, init)

    ms, accs = list(ms), list(accs)
    for u, (hd, c) in enumerate(units):
        if c == 1:
            scores_to(s_odd, 2 * qi + 1, u)
    for u, (hd, c) in enumerate(units):
        ms[u], accs[u] = softmax_pv(s_even, 2 * qi, u, ms[u], accs[u], smax[u], diagonal=(c == 0))
        smax0[u] = scores_to(s_even, 0, u, qn_ref)
    for u, (hd, c) in enumerate(units):
        if c == 1:
            _, accs[u] = softmax_pv(s_odd, 2 * qi + 1, u, ms[u], accs[u], None, diagonal=True)
    for c in range(2):
        o_t = jnp.concatenate([accs[u][:V_HEAD] / accs[u][V_HEAD:V_HEAD + 1]
                               for u, (hd, cc) in enumerate(units) if cc == c], axis=0)
        o_ref[0, c * BK:(c + 1) * BK, :] = o_t.T.astype(BF16)


def _mla_attn(qt, k, vt):
    b, t, _ = k.shape
    assert BQ == 2 * BK
    nh = ATTN_HEADS_PER_STEP
    nq = t // BQ
    return pl.pallas_call(
        _flash_kernel,
        out_shape=jax.ShapeDtypeStruct((b, t, MLA_HEADS * V_HEAD), BF16),
        grid=(b, MLA_HEADS // nh, nq),
        in_specs=[
            pl.BlockSpec((1, nh * HEAD_TILE, BQ), lambda i, hp, j: (i, hp, j)),
            pl.BlockSpec((1, nh * HEAD_TILE, BQ), lambda i, hp, j: (i, hp, jnp.minimum(j + 1, nq - 1))),
            pl.BlockSpec((1, t, nh * HEAD_TILE), lambda i, hp, j: (i, 0, hp)),
            pl.BlockSpec((1, nh, t // BK, VT_ROWS, BK), lambda i, hp, j: (i, hp, 0, 0, 0)),
        ],
        out_specs=pl.BlockSpec((1, BQ, nh * V_HEAD), lambda i, hp, j: (i, j, hp)),
        scratch_shapes=[pltpu.VMEM((2 * nh, BK, BK), F32), pltpu.VMEM((2 * nh, BK, BK), F32),
                        pltpu.VMEM((2 * nh, 1, BK), F32)],
        compiler_params=_cparams(("arbitrary", "arbitrary", "arbitrary")),
        name="mla_attn",
    )(qt, qt, k, vt)


def _head_tile_lanes():
    half = QK_ROPE // 2
    nope = np.concatenate([np.arange(ROPE_LANE0), HEAD_TILE // 2 + np.arange(QK_NOPE - ROPE_LANE0)])
    x1 = ROPE_LANE0 + np.arange(half)
    return np.concatenate([nope, x1, x1 + HEAD_TILE // 2])


def _to_head_tile(w):
    tile = jnp.zeros(w.shape[:-1] + (HEAD_TILE,), F32)
    return tile.at[..., _head_tile_lanes()].set(w)


def _mla_weights(w_dq_dkv, w_uq, w_ukv):
    n = w_dq_dkv.shape[0]
    kpe_w = w_dq_dkv[..., Q_LORA + KV_LORA:]
    kpe_tile = _to_head_tile(jnp.concatenate([jnp.zeros((n, D_MODEL, QK_NOPE), F32), kpe_w], axis=-1))
    wd = jnp.concatenate([w_dq_dkv[..., :Q_LORA + KV_LORA], kpe_tile], axis=-1)
    wuq = _to_head_tile(w_uq.reshape(n, Q_LORA, MLA_HEADS, QK_NOPE + QK_ROPE))
    wuq = wuq.reshape(n, Q_LORA, MLA_HEADS * HEAD_TILE).transpose(0, 2, 1)
    wkv = w_ukv.reshape(n, KV_LORA, MLA_HEADS, QK_NOPE + V_HEAD)
    wuk = _to_head_tile(jnp.pad(wkv[..., :QK_NOPE], ((0, 0), (0, 0), (0, 0), (0, QK_ROPE))))
    wuk = wuk.reshape(n, KV_LORA, MLA_HEADS * HEAD_TILE)
    wuv_t = wkv[..., QK_NOPE:].reshape(n, KV_LORA, MLA_HEADS * V_HEAD).transpose(0, 2, 1)
    return tuple(w.astype(BF16) for w in (wd, wuq, wuk, wuv_t))


def _rope_constants():
    half = QK_ROPE // 2
    inv = 1.0 / (ROPE_THETA ** (np.arange(0, QK_ROPE, 2, dtype=np.float32) / QK_ROPE))
    lanes = _head_tile_lanes()
    x1, x2 = lanes[QK_NOPE:QK_NOPE + half], lanes[QK_NOPE + half:]
    place = np.zeros((2 * half, 2 * HEAD_TILE), np.float32)
    for f in range(half):
        place[f, x1[f]] = 1.0
        place[f, x2[f]] = 1.0
        place[half + f, HEAD_TILE + x1[f]] = -1.0
        place[half + f, HEAD_TILE + x2[f]] = 1.0
    return (jnp.asarray(inv.reshape(half, 1)), jnp.asarray(np.tile(place, (3, 1)), dtype=BF16))


def kernel(x, mem, positions, norm_mix_g, norm_xa_g, norm_mem_g, xa_wq, xa_wkv, xa_wo, norm_ffn_g, ffn_w_up, ffn_conv_w, ffn_conv_b, ffn_w_down, pc_w_in, pool_w, pool_scale, conv_dw_w, conv_dw_b, conv_ln_g, conv_ln_b, pc_w_out, mla_w_dq_dkv, mla_q_norm_g, mla_w_uq, mla_kv_norm_g, mla_w_ukv, mla_w_o, final_norm_g):
    depth = xa_wq.shape[0]
    b, t, _ = x.shape
    bf = lambda a: a.astype(BF16)
    mem_k, mem_v = _mem_kv(mem, norm_mem_g, bf(xa_wkv))
    xa = (_rows(norm_xa_g), bf(xa_wq), mem_k, mem_v, bf(xa_wo))
    ffn = (_rows(norm_ffn_g), bf(ffn_w_up), ffn_conv_w, _rows(ffn_conv_b), bf(ffn_w_down))
    mix_g = _rows(norm_mix_g)
    n_even = pc_w_in.shape[0]
    pc = (bf(pc_w_in), bf(pool_w), _rows(pool_scale),
          jnp.broadcast_to(conv_dw_w[:, :, None, :], (n_even, CONV_K, SUBLANES, CONV_W)),
          _rows(conv_dw_b), _rows(conv_ln_g), _rows(conv_ln_b), bf(pc_w_out))
    wd, wuq, wuk, wuv_t = _mla_weights(mla_w_dq_dkv, mla_w_uq, mla_w_ukv)
    mla = (wd, _rows(mla_q_norm_g), wuq, _rows(mla_kv_norm_g), wuk, wuv_t)
    mla_w_o_b = bf(mla_w_o)
    pos_rows = positions.reshape(b, 1, t)
    inv_col, place = _rope_constants()

    for l in range(depth):
        if l % 2 == 0:
            x = _pc_mixer(x, l, mix_g, l // 2, *pc)
            x = _xattn(x, l, *xa)
        else:
            q, k, vt = _mla_proj(x, pos_rows, inv_col, place, l, mix_g, l // 2, *mla)
            attn = _mla_attn(q, k, vt)
            x = _xattn(x, l, *xa, attn_o=attn, wo_mla=mla_w_o_b, mla_layer=l // 2)
        x = _ffn(x, l, *ffn, final_norm_g, final_norm=(l == depth - 1))
    return x
```

```python
import functools
import math

import jax
import jax.numpy as jnp
import numpy as np
from jax import lax
from jax.experimental import pallas as pl
from jax.experimental.pallas import tpu as pltpu

D_MODEL = 1024
MEM_LEN = 256
XA_HEADS = 4
XA_HEAD_DIM = D_MODEL // XA_HEADS
POOL_W = 512
POOL_WINDOWS = (2, 4, 8, 16)
POOL_GROUP = 128
CONV_W = 512
CONV_K = 31
MLA_HEADS = 16
QK_NOPE = 64
QK_ROPE = 32
V_HEAD = 64
Q_LORA = 384
KV_LORA = 256
ROPE_THETA = 10000.0
MLA_SCALE = 1.0 / math.sqrt(QK_NOPE + QK_ROPE)
D_FF = 2816
FFN_CONV_K = 3
EPS = 1e-6
NEG = -1e30
LOG2E = 1.4426950408889634

LANES = 128
SUBLANES = 8
VMEM_LIMIT = 56 * 1024 * 1024

TB = 512
TB_FFN = 512
TB_XA = 1024
TB_PROJ = 1024
FF_CHUNK = 256
POOL_HALO = 16
CONV_HALO = 32
CONV_ROWS = 64
BK = 512
BQ = 2 * BK
HEAD_TILE = LANES
ROPE_LANE0 = 48
VT_ROWS = V_HEAD + 16

BF16 = jnp.bfloat16
F32 = jnp.float32


def _cparams(sem):
    return pltpu.CompilerParams(dimension_semantics=sem, vmem_limit_bytes=VMEM_LIMIT)


def _rms(x, g):
    return x * lax.rsqrt(jnp.mean(x * x, axis=-1, keepdims=True) + EPS) * g


def _dot(a, b):
    return jnp.dot(a, b, preferred_element_type=F32)


def _dot_nt(a, b):
    return lax.dot_general(a, b, (((1,), (1,)), ((), ())), preferred_element_type=F32)


def _const_spec(shape):
    n = len(shape)
    return pl.BlockSpec(shape, lambda *_: (0,) * n, pipeline_mode=pl.Buffered(1))


def _layer_spec(stacked, layer):
    shape = stacked.shape[1:]
    n = len(shape)
    return pl.BlockSpec((None,) + shape, lambda *_: (layer,) + (0,) * n,
                        pipeline_mode=pl.Buffered(1))


def _rows(stacked):
    return stacked.reshape(stacked.shape[0], 1, stacked.shape[1])


def _mem_kv_kernel(mem_ref, g_ref, wkv_ref, k_ref, v_ref):
    m = _rms(mem_ref[0], g_ref[0]).astype(BF16)
    kv = _dot(m, wkv_ref[0])
    k_ref[0, 0] = kv[:, :D_MODEL].T.astype(BF16)
    v_ref[0, 0] = kv[:, D_MODEL:].astype(BF16)


def _mem_kv(mem, norm_mem_g, xa_wkv):
    depth = xa_wkv.shape[0]
    b = mem.shape[0]
    out = jax.ShapeDtypeStruct((depth, b, MEM_LEN, D_MODEL), BF16)
    out_t = jax.ShapeDtypeStruct((depth, b, D_MODEL, MEM_LEN), BF16)
    return pl.pallas_call(
        _mem_kv_kernel,
        out_shape=(out_t, out),
        grid=(depth, b),
        in_specs=[
            pl.BlockSpec((1, MEM_LEN, D_MODEL), lambda l, i: (i, 0, 0)),
            pl.BlockSpec((1, 1, D_MODEL), lambda l, i: (l, 0, 0)),
            pl.BlockSpec((1, D_MODEL, 2 * D_MODEL), lambda l, i: (l, 0, 0)),
        ],
        out_specs=(
            pl.BlockSpec((1, 1, D_MODEL, MEM_LEN), lambda l, i: (l, i, 0, 0)),
            pl.BlockSpec((1, 1, MEM_LEN, D_MODEL), lambda l, i: (l, i, 0, 0)),
        ),
        compiler_params=_cparams(("arbitrary", "arbitrary")),
        name="mem_kv",
    )(mem, norm_mem_g.reshape(depth, 1, D_MODEL), xa_wkv)


def _xattn_kernel(*refs, with_attn_proj):
    if with_attn_proj:
        x_ref, o_ref, wo_mla_ref, g_ref, wq_ref, k_ref, v_ref, wo_ref, out_ref = refs
        x = x_ref[0] + _dot(o_ref[0], wo_mla_ref[...])
    else:
        x_ref, g_ref, wq_ref, k_ref, v_ref, wo_ref, out_ref = refs
        x = x_ref[0]
    h = _rms(x, g_ref[...]).astype(BF16)
    q = (_dot(h, wq_ref[...]) * (XA_HEAD_DIM ** -0.5)).astype(BF16)
    heads = []
    for hd in range(XA_HEADS):
        sl = slice(hd * XA_HEAD_DIM, (hd + 1) * XA_HEAD_DIM)
        s = _dot(q[:, sl], k_ref[0, sl, :])
        m = jnp.max(s, axis=-1, keepdims=True)
        p = jnp.exp(s - m)
        l = jnp.sum(p, axis=-1, keepdims=True)
        o = _dot(p.astype(BF16), v_ref[0, :, sl]) / l
        heads.append(o.astype(BF16))
    o = jnp.concatenate(heads, axis=-1)
    out_ref[0] = x + _dot(o, wo_ref[...])


def _xattn(x, layer, g, wq, k, v, wo, attn_o=None, wo_mla=None, mla_layer=None):
    b, t, _ = x.shape
    with_attn_proj = attn_o is not None
    tok = pl.BlockSpec((1, TB_XA, D_MODEL), lambda i, j: (i, j, 0))
    mem = pl.BlockSpec((None, 1, MEM_LEN, D_MODEL), lambda i, j: (layer, i, 0, 0))
    mem_t = pl.BlockSpec((None, 1, D_MODEL, MEM_LEN), lambda i, j: (layer, i, 0, 0))
    in_specs = [tok]
    args = [x]
    if with_attn_proj:
        in_specs += [tok, _layer_spec(wo_mla, mla_layer)]
        args += [attn_o, wo_mla]
    in_specs += [_layer_spec(g, layer), _layer_spec(wq, layer), mem_t, mem, _layer_spec(wo, layer)]
    args += [g, wq, k, v, wo]
    return pl.pallas_call(
        functools.partial(_xattn_kernel, with_attn_proj=with_attn_proj),
        out_shape=jax.ShapeDtypeStruct(x.shape, F32),
        grid=(b, t // TB_XA),
        in_specs=in_specs,
        out_specs=tok,
        compiler_params=_cparams(("arbitrary", "arbitrary")),
        name="xattn",
    )(*args)


def _ffn_kernel(x_ref, g_ref, wup_ref, cw_ref, cb_ref, wdn_ref, fg_ref, out_ref,
                tail, act, *, final_norm, tb):
    n_chunks = D_FF // FF_CHUNK

    @pl.when(pl.program_id(1) == 0)
    def _():
        tail[...] = jnp.zeros_like(tail)

    def delayed(gt, prev, d):
        row = lax.broadcasted_iota(jnp.int32, (SUBLANES, FF_CHUNK), 0)
        rolled = pltpu.roll(gt, d, 0)
        head = jnp.where(row < d, pltpu.roll(prev, d, 0), rolled[:SUBLANES])
        return jnp.concatenate([head, rolled[SUBLANES:]], axis=0)

    x = x_ref[0]
    h = _rms(x, g_ref[...]).astype(BF16)
    for c in range(n_chunks):
        cs = slice(c * FF_CHUNK, (c + 1) * FF_CHUNK)
        gs = slice(D_FF + c * FF_CHUNK, D_FF + (c + 1) * FF_CHUNK)
        a = _dot(h, wup_ref[:, cs])
        gt = _dot(h, wup_ref[:, gs])
        prev = tail[:, cs]
        tail[:, cs] = gt[tb - SUBLANES:]
        cw = cw_ref[:, cs]
        conv = (cw[0:1] * delayed(gt, prev, 2) + cw[1:2] * delayed(gt, prev, 1)
                + cw[2:3] * gt + cb_ref[:, cs])
        act[:, cs] = (conv * jax.nn.sigmoid(conv) * a).astype(BF16)
    y = x + _dot(act[...], wdn_ref[...])
    if final_norm:
        y = _rms(y, fg_ref[...])
    out_ref[0] = y


def _ffn(x, layer, g, w_up, conv_w, conv_b, w_down, final_g, final_norm):
    b, t, _ = x.shape
    tb = TB_FFN
    tok = pl.BlockSpec((1, tb, D_MODEL), lambda i, j: (i, j, 0))
    return pl.pallas_call(
        functools.partial(_ffn_kernel, final_norm=final_norm, tb=tb),
        out_shape=jax.ShapeDtypeStruct(x.shape, F32),
        grid=(b, t // tb),
        in_specs=[
            tok,
            _layer_spec(g, layer),
            _layer_spec(w_up, layer),
            _layer_spec(conv_w, layer),
            _layer_spec(conv_b, layer),
            _layer_spec(w_down, layer),
            _const_spec((1, D_MODEL)),
        ],
        out_specs=tok,
        scratch_shapes=[
            pltpu.VMEM((SUBLANES, D_FF), F32),
            pltpu.VMEM((tb, D_FF), BF16),
        ],
        compiler_params=_cparams(("arbitrary", "arbitrary")),
        name="conv_ffn",
    )(x, g, w_up, conv_w, conv_b, w_down, final_g.reshape(1, D_MODEL))


def _pc_kernel(x_ref, g_ref, win_ref, pw_ref, ps_ref, dw_ref, db_ref, lg_ref, lb_ref, wout_ref,
               out_ref, ubuf, glbuf, shbuf, ybuf):
    tblk = pl.program_id(1)

    @pl.when(tblk == 0)
    def _():
        ubuf[0:POOL_HALO, :] = jnp.zeros((POOL_HALO, POOL_W), F32)
        glbuf[0:CONV_HALO, :] = jnp.zeros((CONV_HALO, CONV_W), F32)

    x = x_ref[0]
    h = _rms(x, g_ref[...]).astype(BF16)
    z = _dot(h, win_ref[...])
    u = z[:, :POOL_W]
    gl = z[:, POOL_W:POOL_W + CONV_W] * jax.nn.sigmoid(z[:, POOL_W + CONV_W:])
    ubuf[POOL_HALO:POOL_HALO + TB, :] = u
    glbuf[CONV_HALO:CONV_HALO + TB, :] = gl

    pos = tblk * TB + lax.broadcasted_iota(jnp.int32, (TB, 1), 0)
    for gi, w in enumerate(POOL_WINDOWS):
        ls = slice(gi * POOL_GROUP, (gi + 1) * POOL_GROUP)
        ue = ubuf[:, ls]
        s8 = ue
        for d in range(1, min(w, SUBLANES)):
            s8 = s8 + pltpu.roll(ue, d, 0)
        ssum = s8[POOL_HALO:]
        if w > SUBLANES:
            assert w == 2 * SUBLANES
            ssum = ssum + s8[POOL_HALO - SUBLANES:POOL_HALO - SUBLANES + TB]
        cnt = jnp.minimum(pos + 1, w).astype(F32)
        pooled = (ssum / cnt - u[:, ls]).astype(BF16)
        ya = _dot(pooled, pw_ref[gi]) * ps_ref[:, ls]
        ybuf[:, ls] = ya.astype(BF16)

    lo = SUBLANES
    gle = glbuf[...]
    for r in range(1, SUBLANES):
        shbuf[r - 1, lo:CONV_HALO + TB, :] = pltpu.roll(gle, r, 0)[lo:]
    tiles = CONV_ROWS // SUBLANES
    for c0 in range(0, TB, CONV_ROWS):
        cv = jnp.zeros((tiles, SUBLANES, CONV_W), F32) + db_ref[...]
        for j in range(CONV_K):
            a, r = divmod(CONV_K - 1 - j, SUBLANES)
            start = CONV_HALO - SUBLANES * a + c0
            src = glbuf if r == 0 else shbuf.at[r - 1]
            win = src[start:start + CONV_ROWS, :].reshape(tiles, SUBLANES, CONV_W)
            cv = cv + dw_ref[j] * win
        mu = jnp.mean(cv, axis=-1, keepdims=True)
        xc = cv - mu
        yn = (xc * lax.rsqrt(jnp.mean(xc * xc, axis=-1, keepdims=True) + EPS) * lg_ref[...]
              + lb_ref[...])
        act = (yn * jax.nn.sigmoid(yn)).reshape(CONV_ROWS, CONV_W)
        ybuf[c0:c0 + CONV_ROWS, POOL_W:] = act.astype(BF16)

    ubuf[0:POOL_HALO, :] = ubuf[TB:TB + POOL_HALO, :]
    glbuf[0:CONV_HALO, :] = glbuf[TB:TB + CONV_HALO, :]

    out_ref[0] = x + _dot(ybuf[...], wout_ref[...])


def _pc_mixer(x, layer, g, e, w_in, pool_w, pool_scale, dw_w, dw_b, ln_g, ln_b, w_out):
    b, t, _ = x.shape
    tok = pl.BlockSpec((1, TB, D_MODEL), lambda i, j: (i, j, 0))
    params = (w_in, pool_w, pool_scale, dw_w, dw_b, ln_g, ln_b, w_out)
    return pl.pallas_call(
        _pc_kernel,
        out_shape=jax.ShapeDtypeStruct(x.shape, F32),
        grid=(b, t // TB),
        in_specs=[tok, _layer_spec(g, layer)] + [_layer_spec(p, e) for p in params],
        out_specs=tok,
        scratch_shapes=[
            pltpu.VMEM((POOL_HALO + TB, POOL_W), F32),
            pltpu.VMEM((CONV_HALO + TB, CONV_W), F32),
            pltpu.VMEM((SUBLANES - 1, CONV_HALO + TB, CONV_W), F32),
            pltpu.VMEM((TB, POOL_W + CONV_W), BF16),
        ],
        compiler_params=_cparams(("arbitrary", "arbitrary")),
        name="pc_mixer",
    )(x, g, *params)


def _rope_tables(pos_row, inv_col, place):
    ang = inv_col * pos_row.astype(F32)
    cs = jnp.concatenate([jnp.cos(ang), jnp.sin(ang)], axis=0)
    hi = cs.astype(BF16)
    r1 = cs - hi.astype(F32)
    mid = r1.astype(BF16)
    lo = (r1 - mid.astype(F32)).astype(BF16)
    pieces = jnp.concatenate([hi, mid, lo], axis=0)
    t = lax.dot_general(pieces, place, (((0,), (0,)), ((), ())), preferred_element_type=F32)
    lane = lax.broadcasted_iota(jnp.int32, (1, HEAD_TILE), 1)
    is_rope = (lane % (HEAD_TILE // 2)) >= ROPE_LANE0
    cos_t = t[:, :HEAD_TILE] + jnp.where(is_rope, 0.0, 1.0)
    half = QK_ROPE // 2
    cos_f, sin_f = cs[:half], cs[half:]
    gap = HEAD_TILE // 2 - half
    fill = lambda v: jnp.full((gap, cs.shape[1]), v, F32)
    cos_tt = jnp.concatenate([fill(1.0), cos_f, fill(1.0), cos_f], axis=0)
    sin_tt = jnp.concatenate([fill(0.0), -sin_f, fill(0.0), sin_f], axis=0)
    return cos_t, t[:, HEAD_TILE:], cos_tt, sin_tt


def _rope_tile(x, cos_t, sin_t):
    return x * cos_t + pltpu.roll(x, HEAD_TILE // 2, 1) * sin_t


def _mla_proj_kernel(x_ref, pos_ref, inv_ref, place_ref, g_ref, wd_ref, qg_ref, wuq_ref, kvg_ref,
                     wuk_ref, wuv_ref, q_ref, k_ref, v_ref):
    x = x_ref[0]
    h = _rms(x, g_ref[...]).astype(BF16)
    c = _dot(h, wd_ref[...])
    cq = _rms(c[:, :Q_LORA], qg_ref[...]).astype(BF16)
    ckv = _rms(c[:, Q_LORA:Q_LORA + KV_LORA], kvg_ref[...]).astype(BF16)
    kpe = c[:, Q_LORA + KV_LORA:]

    cos_t, sin_t, cos_tt, sin_tt = _rope_tables(pos_ref[0], inv_ref[...], place_ref[...])
    kpe = _rope_tile(kpe, cos_t, sin_t)
    qscale = MLA_SCALE * LOG2E
    cos_q, sin_q = cos_tt * qscale, sin_tt * qscale
    qt = _dot_nt(wuq_ref[...], cq)
    kn = _dot(ckv, wuk_ref[...])
    hh = HEAD_TILE // 2
    for hd in range(MLA_HEADS):
        sl = slice(hd * HEAD_TILE, (hd + 1) * HEAD_TILE)
        qh = qt[sl]
        swapped = jnp.concatenate([qh[hh:], qh[:hh]], axis=0)
        q_ref[0, sl, :] = (qh * cos_q + swapped * sin_q).astype(BF16)
        k_ref[0, :, sl] = (kn[:, sl] + kpe).astype(BF16)
    vt = _dot_nt(wuv_ref[...], ckv)
    ones_tile = jnp.where(lax.broadcasted_iota(jnp.int32, (VT_ROWS - V_HEAD, BK), 0) == 0, 1.0, 0.0)
    for hd in range(MLA_HEADS):
        for kb in range(TB_PROJ // BK):
            v_ref[0, hd, kb, 0:V_HEAD, :] = vt[hd * V_HEAD:(hd + 1) * V_HEAD,
                                               kb * BK:(kb + 1) * BK].astype(BF16)
            v_ref[0, hd, kb, V_HEAD:VT_ROWS, :] = ones_tile.astype(BF16)


def _mla_proj(x, pos_rows, inv_col, place, layer, g, o, wd, qg, wuq, kvg, wuk, wuv_t):
    b, t, _ = x.shape
    tb = TB_PROJ
    hw = MLA_HEADS * HEAD_TILE
    tok = pl.BlockSpec((1, tb, D_MODEL), lambda i, j: (i, j, 0))
    wide = pl.BlockSpec((1, tb, hw), lambda i, j: (i, j, 0))
    return pl.pallas_call(
        _mla_proj_kernel,
        out_shape=(
            jax.ShapeDtypeStruct((b, hw, t), BF16),
            jax.ShapeDtypeStruct((b, t, hw), BF16),
            jax.ShapeDtypeStruct((b, MLA_HEADS, t // BK, VT_ROWS, BK), BF16),
        ),
        grid=(b, t // tb),
        in_specs=[
            tok,
            pl.BlockSpec((1, 1, tb), lambda i, j: (i, 0, j)),
            _const_spec(inv_col.shape),
            _const_spec(place.shape),
            _layer_spec(g, layer),
        ] + [_layer_spec(p, o) for p in (wd, qg, wuq, kvg, wuk, wuv_t)],
        out_specs=(pl.BlockSpec((1, hw, tb), lambda i, j: (i, 0, j)), wide,
                   pl.BlockSpec((1, MLA_HEADS, tb // BK, VT_ROWS, BK), lambda i, j: (i, 0, j, 0, 0))),
        compiler_params=_cparams(("arbitrary", "arbitrary")),
        name="mla_proj",
    )(x, pos_rows, inv_col, place, g, wd, qg, wuq, kvg, wuk, wuv_t)


ATTN_HEADS_PER_STEP = 4


def _flash_kernel(q_ref, qn_ref, k_ref, vt_ref, o_ref, s_even, s_odd, smax0):
    qi = pl.program_id(2)
    nh = ATTN_HEADS_PER_STEP
    units = [(hd, c) for hd in range(nh) for c in range(2)]

    def scores_to(buf, j, u, queries=q_ref):
        hd, c = units[u]
        lanes = slice(hd * HEAD_TILE, (hd + 1) * HEAD_TILE)
        k0 = pl.multiple_of(j * BK, BK)
        s = _dot(k_ref[0, pl.ds(k0, BK), lanes], queries[0, lanes, c * BK:(c + 1) * BK])
        buf[u] = s
        return jnp.max(s, axis=0, keepdims=True)

    @pl.when(qi == 0)
    def _():
        for u in range(len(units)):
            smax0[u] = scores_to(s_even, 0, u)

    def online_update(s, vt, m, acc, smax):
        m_new = jnp.maximum(m, smax)
        alpha = jnp.exp2(m - m_new)
        p = jnp.exp2(s - m_new).astype(BF16)
        return m_new, alpha * acc + _dot(vt, p)

    def softmax_pv(buf, j, u, m, acc, smax, diagonal=False):
        vt = vt_ref[0, units[u][0], j]
        if not diagonal:
            return online_update(buf[u], vt, m, acc, smax)
        hk = BK // 2
        new = []
        for keys, q0 in ((hk, 0), (BK, hk)):
            s = buf[u, 0:keys, q0:q0 + hk]
            kv = lax.broadcasted_iota(jnp.int32, (keys, hk), 0)
            qq = lax.broadcasted_iota(jnp.int32, (keys, hk), 1) + q0
            s = jnp.where(kv <= qq, s, NEG)
            new.append(online_update(s, vt[:, 0:keys], m[:, q0:q0 + hk], acc[:, q0:q0 + hk],
                                     jnp.max(s, axis=0, keepdims=True)))
        return (jnp.concatenate([new[0][0], new[1][0]], axis=1),
                jnp.concatenate([new[0][1], new[1][1]], axis=1))

    def half_step(cur, nxt, j, carry):
        ms, accs, smax = (list(t) for t in carry)
        smax_next = []
        for u in range(len(units)):
            smax_next.append(scores_to(nxt, j + 1, u))
            ms[u], accs[u] = softmax_pv(cur, j, u, ms[u], accs[u], smax[u])
        return tuple(ms), tuple(accs), tuple(smax_next)

    def body(jj, carry):
        carry = half_step(s_even, s_odd, 2 * jj, carry)
        return half_step(s_odd, s_even, 2 * jj + 1, carry)

    init = (tuple(jnp.full((1, BK), NEG, F32) for _ in units),
            tuple(jnp.zeros((VT_ROWS, BK), F32) for _ in units),
            tuple(smax0[u] for u in range(len(units))))
    ms, accs, smax = lax.fori_loop(0, qi, body, init)

    ms, accs = list(ms), list(accs)
    for u, (hd, c) in enumerate(units):
        if c == 1:
            scores_to(s_odd, 2 * qi + 1, u)
    for u, (hd, c) in enumerate(units):
        ms[u], accs[u] = softmax_pv(s_even, 2 * qi, u, ms[u], accs[u], smax[u], diagonal=(c == 0))
        smax0[u] = scores_to(s_even, 0, u, qn_ref)
    for u, (hd, c) in enumerate(units):
        if c == 1:
            _, accs[u] = softmax_pv(s_odd, 2 * qi + 1, u, ms[u], accs[u], None, diagonal=True)
    for c in range(2):
        o_t = jnp.concatenate([accs[u][:V_HEAD] / accs[u][V_HEAD:V_HEAD + 1]
                               for u, (hd, cc) in enumerate(units) if cc == c], axis=0)
        o_ref[0, c * BK:(c + 1) * BK, :] = o_t.T.astype(BF16)


def _mla_attn(qt, k, vt):
    b, t, _ = k.shape
    assert BQ == 2 * BK
    nh = ATTN_HEADS_PER_STEP
    nq = t // BQ
    return pl.pallas_call(
        _flash_kernel,
        out_shape=jax.ShapeDtypeStruct((b, t, MLA_HEADS * V_HEAD), BF16),
        grid=(b, MLA_HEADS // nh, nq),
        in_specs=[
            pl.BlockSpec((1, nh * HEAD_TILE, BQ), lambda i, hp, j: (i, hp, j)),
            pl.BlockSpec((1, nh * HEAD_TILE, BQ), lambda i, hp, j: (i, hp, jnp.minimum(j + 1, nq - 1))),
            pl.BlockSpec((1, t, nh * HEAD_TILE), lambda i, hp, j: (i, 0, hp)),
            pl.BlockSpec((1, nh, t // BK, VT_ROWS, BK), lambda i, hp, j: (i, hp, 0, 0, 0)),
        ],
        out_specs=pl.BlockSpec((1, BQ, nh * V_HEAD), lambda i, hp, j: (i, j, hp)),
        scratch_shapes=[pltpu.VMEM((2 * nh, BK, BK), F32), pltpu.VMEM((2 * nh, BK, BK), F32),
                        pltpu.VMEM((2 * nh, 1, BK), F32)],
        compiler_params=_cparams(("arbitrary", "arbitrary", "arbitrary")),
        name="mla_attn",
    )(qt, qt, k, vt)


def _head_tile_lanes():
    half = QK_ROPE // 2
    nope = np.concatenate([np.arange(ROPE_LANE0), HEAD_TILE // 2 + np.arange(QK_NOPE - ROPE_LANE0)])
    x1 = ROPE_LANE0 + np.arange(half)
    return np.concatenate([nope, x1, x1 + HEAD_TILE // 2])


def _to_head_tile(w):
    tile = jnp.zeros(w.shape[:-1] + (HEAD_TILE,), F32)
    return tile.at[..., _head_tile_lanes()].set(w)


def _mla_weights(w_dq_dkv, w_uq, w_ukv):
    n = w_dq_dkv.shape[0]
    kpe_w = w_dq_dkv[..., Q_LORA + KV_LORA:]
    kpe_tile = _to_head_tile(jnp.concatenate([jnp.zeros((n, D_MODEL, QK_NOPE), F32), kpe_w], axis=-1))
    wd = jnp.concatenate([w_dq_dkv[..., :Q_LORA + KV_LORA], kpe_tile], axis=-1)
    wuq = _to_head_tile(w_uq.reshape(n, Q_LORA, MLA_HEADS, QK_NOPE + QK_ROPE))
    wuq = wuq.reshape(n, Q_LORA, MLA_HEADS * HEAD_TILE).transpose(0, 2, 1)
    wkv = w_ukv.reshape(n, KV_LORA, MLA_HEADS, QK_NOPE + V_HEAD)
    wuk = _to_head_tile(jnp.pad(wkv[..., :QK_NOPE], ((0, 0), (0, 0), (0, 0), (0, QK_ROPE))))
    wuk = wuk.reshape(n, KV_LORA, MLA_HEADS * HEAD_TILE)
    wuv_t = wkv[..., QK_NOPE:].reshape(n, KV_LORA, MLA_HEADS * V_HEAD).transpose(0, 2, 1)
    return tuple(w.astype(BF16) for w in (wd, wuq, wuk, wuv_t))


def _rope_constants():
    half = QK_ROPE // 2
    inv = 1.0 / (ROPE_THETA ** (np.arange(0, QK_ROPE, 2, dtype=np.float32) / QK_ROPE))
    lanes = _head_tile_lanes()
    x1, x2 = lanes[QK_NOPE:QK_NOPE + half], lanes[QK_NOPE + half:]
    place = np.zeros((2 * half, 2 * HEAD_TILE), np.float32)
    for f in range(half):
        place[f, x1[f]] = 1.0
        place[f, x2[f]] = 1.0
        place[half + f, HEAD_TILE + x1[f]] = -1.0
        place[half + f, HEAD_TILE + x2[f]] = 1.0
    return (jnp.asarray(inv.reshape(half, 1)), jnp.asarray(np.tile(place, (3, 1)), dtype=BF16))


def kernel(x, mem, positions, norm_mix_g, norm_xa_g, norm_mem_g, xa_wq, xa_wkv, xa_wo, norm_ffn_g, ffn_w_up, ffn_conv_w, ffn_conv_b, ffn_w_down, pc_w_in, pool_w, pool_scale, conv_dw_w, conv_dw_b, conv_ln_g, conv_ln_b, pc_w_out, mla_w_dq_dkv, mla_q_norm_g, mla_w_uq, mla_kv_norm_g, mla_w_ukv, mla_w_o, final_norm_g):
    depth = xa_wq.shape[0]
    b, t, _ = x.shape
    bf = lambda a: a.astype(BF16)
    mem_k, mem_v = _mem_kv(mem, norm_mem_g, bf(xa_wkv))
    xa = (_rows(norm_xa_g), bf(xa_wq), mem_k, mem_v, bf(xa_wo))
    ffn = (_rows(norm_ffn_g), bf(ffn_w_up), ffn_conv_w, _rows(ffn_conv_b), bf(ffn_w_down))
    mix_g = _rows(norm_mix_g)
    n_even = pc_w_in.shape[0]
    pc = (bf(pc_w_in), bf(pool_w), _rows(pool_scale),
          jnp.broadcast_to(conv_dw_w[:, :, None, :], (n_even, CONV_K, SUBLANES, CONV_W)),
          _rows(conv_dw_b), _rows(conv_ln_g), _rows(conv_ln_b), bf(pc_w_out))
    wd, wuq, wuk, wuv_t = _mla_weights(mla_w_dq_dkv, mla_w_uq, mla_w_ukv)
    mla = (wd, _rows(mla_q_norm_g), wuq, _rows(mla_kv_norm_g), wuk, wuv_t)
    mla_w_o_b = bf(mla_w_o)
    pos_rows = positions.reshape(b, 1, t)
    inv_col, place = _rope_constants()

    for l in range(depth):
        if l % 2 == 0:
            x = _pc_mixer(x, l, mix_g, l // 2, *pc)
            x = _xattn(x, l, *xa)
        else:
            q, k, vt = _mla_proj(x, pos_rows, inv_col, place, l, mix_g, l // 2, *mla)
            attn = _mla_attn(q, k, vt)
            x = _xattn(x, l, *xa, attn_o=attn, wo_mla=mla_w_o_b, mla_layer=l // 2)
        x = _ffn(x, l, *ffn, final_norm_g, final_norm=(l == depth - 1))
    return x
```
